```python
import math
import jax, jax.numpy as jnp
from jax import lax
import numpy as np

D_MODEL = 1024
BATCH = 16
SEQ = 2048
DEPTH = 4

HEAD_DIM = 64
N_SB = 4
N_MOBA = 4
N_FOX = 4
N_DSA = 4
BRANCH_W = 4 * HEAD_DIM
N_BRANCH = 4
Q_BLOCK = 128
MOBA_BLOCK = 256
MOBA_TOPK = 3
MOBA_Q_CHUNK = 32
DSA_TOPK = 256
IDX_HEADS = 8
IDX_DIM = 32
T5_BUCKETS = 32
T5_MAX_DIST = 128
D_FF = 2816
CONV_W = 3
RMS_EPS = 1e-6
NEG_BIG = -1e30

IN_SIZES = (
    [BRANCH_W] * 3
    + [BRANCH_W] * 3
    + [BRANCH_W] * 3 + [N_FOX]
    + [BRANCH_W, HEAD_DIM, HEAD_DIM]
    + [IDX_HEADS * IDX_DIM, IDX_DIM, IDX_HEADS]
)
P_IN = int(sum(IN_SIZES))
IN_SPLITS = tuple(int(v) for v in np.cumsum(IN_SIZES)[:-1])

kernel_name = "hybrid_gated_sb_moba_fox_dsa_trunk"


def rmsnorm(x, g):
    xf = x.astype(jnp.float32)
    y = xf * lax.rsqrt(jnp.mean(xf * xf, axis=-1, keepdims=True) + RMS_EPS)
    return (y * g.astype(jnp.float32)).astype(x.dtype)


def to_heads(t, n):
    b, s, _ = t.shape
    return t.reshape(b, s, n, HEAD_DIM).transpose(0, 2, 1, 3)


def merge_heads(o):
    b, h, s, d = o.shape
    return o.transpose(0, 2, 1, 3).reshape(b, s, h * d)


def t5_bucket(n):
    n = jnp.maximum(n, 0)
    max_exact = T5_BUCKETS // 2
    nf = jnp.maximum(n, 1).astype(jnp.float32)
    large = max_exact + (jnp.log(nf / max_exact) / math.log(T5_MAX_DIST / max_exact)
                         * (T5_BUCKETS - max_exact)).astype(jnp.int32)
    large = jnp.minimum(large, T5_BUCKETS - 1)
    return jnp.where(n < max_exact, n, large)


def stick_breaking_attention(q, k, v):
    s_len = q.shape[2]
    scale = HEAD_DIM ** -0.5
    outs = []
    for i in range(s_len // Q_BLOCK):
        t0, t1 = i * Q_BLOCK, (i + 1) * Q_BLOCK
        z = jnp.einsum('bhqd,bhkd->bhqk', q[:, :, t0:t1], k[:, :, :t1]).astype(jnp.float32) * scale
        strict = jnp.arange(t1)[None, :] < jnp.arange(t0, t1)[:, None]
        log_neg = jnp.where(strict, jax.nn.log_sigmoid(-z), 0.0)
        excl = lax.cumsum(log_neg, axis=3, reverse=True) - log_neg
        w = jnp.where(strict, jnp.exp(jax.nn.log_sigmoid(z) + excl), 0.0)
        outs.append(jnp.einsum('bhqk,bhkd->bhqd', w.astype(v.dtype), v[:, :, :t1]))
    return jnp.concatenate(outs, axis=2)


def forgetting_attention(q, k, v, f_logit):
    s_len = q.shape[2]
    scale = HEAD_DIM ** -0.5
    c = jnp.cumsum(jax.nn.log_sigmoid(f_logit.astype(jnp.float32)), axis=1).transpose(0, 2, 1)
    outs = []
    for i in range(s_len // Q_BLOCK):
        t0, t1 = i * Q_BLOCK, (i + 1) * Q_BLOCK
        logits = jnp.einsum('bhqd,bhkd->bhqk', q[:, :, t0:t1], k[:, :, :t1]).astype(jnp.float32) * scale
        logits = logits + c[:, :, t0:t1, None] - c[:, :, None, :t1]
        causal = jnp.arange(t1)[None, :] <= jnp.arange(t0, t1)[:, None]
        p = jax.nn.softmax(jnp.where(causal, logits, NEG_BIG), axis=-1)
        outs.append(jnp.einsum('bhqk,bhkd->bhqd', p.astype(v.dtype), v[:, :, :t1]))
    return jnp.concatenate(outs, axis=2)


def moba_attention(q, k, v, bias_table):
    b, h, s_len, d = q.shape
    scale = HEAD_DIM ** -0.5
    nb = -(-s_len // MOBA_BLOCK)
    pad = nb * MOBA_BLOCK - s_len
    kp = jnp.pad(k, ((0, 0), (0, 0), (0, pad), (0, 0)))
    vp = jnp.pad(v, ((0, 0), (0, 0), (0, pad), (0, 0)))
    kb = kp.reshape(b, h, nb, MOBA_BLOCK, d)
    vb = vp.reshape(b, h, nb, MOBA_BLOCK, d)
    kmean = jnp.mean(kb.astype(jnp.float32), axis=3).astype(k.dtype)
    topk = min(MOBA_TOPK, nb)
    table_t = bias_table.T
    bi = jnp.arange(b)[:, None, None, None]
    hi = jnp.arange(h)[None, :, None, None]

    def chunk(ci):
        t0 = ci * MOBA_Q_CHUNK
        qc = lax.dynamic_slice_in_dim(q, t0, MOBA_Q_CHUNK, axis=2)
        tpos = t0 + jnp.arange(MOBA_Q_CHUNK)
        own = t0 // MOBA_BLOCK
        gate = jnp.einsum('bhqd,bhnd->bhqn', qc, kmean).astype(jnp.float32)
        past = jnp.arange(nb) < own
        gate = jnp.where(past, gate, -jnp.inf)
        _, gidx = lax.top_k(gate, topk)
        sel_valid = gidx < own
        ks = kb[bi, hi, gidx]
        vs = vb[bi, hi, gidx]
        s_sel = jnp.einsum('bhqd,bhqnkd->bhqnk', qc, ks).astype(jnp.float32) * scale
        kpos = gidx[..., None] * MOBA_BLOCK + jnp.arange(MOBA_BLOCK)
        bucket = t5_bucket(tpos[None, None, :, None, None] - kpos)
        s_sel = s_sel + table_t[hi[..., None], bucket].astype(jnp.float32)
        s_sel = jnp.where(sel_valid[..., None], s_sel, NEG_BIG)
        ko = lax.dynamic_slice_in_dim(kp, own * MOBA_BLOCK, MOBA_BLOCK, axis=2)
        vo = lax.dynamic_slice_in_dim(vp, own * MOBA_BLOCK, MOBA_BLOCK, axis=2)
        kpos_own = own * MOBA_BLOCK + jnp.arange(MOBA_BLOCK)
        dist_own = tpos[:, None] - kpos_own[None, :]
        s_own = jnp.einsum('bhqd,bhkd->bhqk', qc, ko).astype(jnp.float32) * scale
        s_own = s_own + table_t[:, t5_bucket(dist_own)].astype(jnp.float32)[None]
        s_own = jnp.where(dist_own >= 0, s_own, NEG_BIG)
        logits = jnp.concatenate([s_sel.reshape(b, h, MOBA_Q_CHUNK, topk * MOBA_BLOCK), s_own], axis=-1)
        p = jax.nn.softmax(logits, axis=-1).astype(v.dtype)
        p_sel = p[..., :topk * MOBA_BLOCK].reshape(b, h, MOBA_Q_CHUNK, topk, MOBA_BLOCK)
        p_own = p[..., topk * MOBA_BLOCK:]
        return (jnp.einsum('bhqnk,bhqnkd->bhqd', p_sel, vs)
                + jnp.einsum('bhqk,bhkd->bhqd', p_own, vo))

    out = lax.map(chunk, jnp.arange(s_len // MOBA_Q_CHUNK))
    return out.transpose(1, 2, 0, 3, 4).reshape(b, h, s_len, d)


def dsa_attention(q, k, v, q_idx, k_idx, w_idx, bias_table):
    b, h, s_len, d = q.shape
    scale = HEAD_DIM ** -0.5
    ksel = min(DSA_TOPK, s_len // 4)
    table_t = bias_table.T
    bi = jnp.arange(b)[:, None, None]
    outs = []
    for i in range(s_len // Q_BLOCK):
        t0, t1 = i * Q_BLOCK, (i + 1) * Q_BLOCK
        kk = min(ksel, t1)
        tpos = jnp.arange(t0, t1)
        isc = jax.nn.relu(jnp.einsum('bqhe,bke->bqhk', q_idx[:, t0:t1], k_idx[:, :t1]))
        isc = jnp.einsum('bqh,bqhk->bqk', w_idx[:, t0:t1], isc).astype(jnp.float32)
        admissible = jnp.arange(t1)[None, :] <= tpos[:, None]
        _, idx = lax.top_k(jnp.where(admissible, isc, -jnp.inf), kk)
        dist = tpos[None, :, None] - idx
        ks = k[bi, idx]
        vs = v[bi, idx]
        logits = jnp.einsum('bhqd,bqkd->bhqk', q[:, :, t0:t1], ks).astype(jnp.float32) * scale
        logits = logits + table_t[:, t5_bucket(dist)].transpose(1, 0, 2, 3).astype(jnp.float32)
        logits = jnp.where((dist >= 0)[:, None], logits, NEG_BIG)
        p = jax.nn.softmax(logits, axis=-1).astype(v.dtype)
        outs.append(jnp.einsum('bhqk,bqkd->bhqd', p, vs))
    return jnp.concatenate(outs, axis=2)


def token_mixer(xn, w_in, fox_b_f, t5_bias, w_gate, w_branch, w_out):
    b, s_len, _ = xn.shape
    proj = xn @ w_in
    (sb_q, sb_k, sb_v, mb_q, mb_k, mb_v, fx_q, fx_k, fx_v, fx_f,
     ds_q, ds_k, ds_v, ix_q, ix_k, ix_w) = jnp.split(proj, IN_SPLITS, axis=-1)
    o_sb = stick_breaking_attention(to_heads(sb_q, N_SB), to_heads(sb_k, N_SB), to_heads(sb_v, N_SB))
    o_mb = moba_attention(to_heads(mb_q, N_MOBA), to_heads(mb_k, N_MOBA), to_heads(mb_v, N_MOBA),
                          t5_bias[:, :N_MOBA])
    o_fx = forgetting_attention(to_heads(fx_q, N_FOX), to_heads(fx_k, N_FOX), to_heads(fx_v, N_FOX),
                                fx_f + fox_b_f)
    o_ds = dsa_attention(to_heads(ds_q, N_DSA), ds_k, ds_v,
                         ix_q.reshape(b, s_len, IDX_HEADS, IDX_DIM), ix_k, ix_w, t5_bias[:, N_MOBA:])
    branches = (o_sb, o_mb, o_fx, o_ds)
    mixed = None
    for i in range(N_BRANCH):
        y = jax.nn.sigmoid(xn @ w_gate[i]) * (merge_heads(branches[i]) @ w_branch[i])
        mixed = y if mixed is None else mixed + y
    return mixed @ w_out


def conv_ffn(xn, w_up, conv_w, conv_b, w_down):
    s_len = xn.shape[1]
    hdn = xn @ w_up
    hp = jnp.pad(hdn, ((0, 0), (CONV_W - 1, 0), (0, 0)))
    acc = conv_b
    for j in range(CONV_W):
        acc = acc + conv_w[j] * hp[:, j:j + s_len]
    g, u = jnp.split(acc, 2, axis=-1)
    return (jax.nn.gelu(g, approximate=False) * u) @ w_down


def setup_inputs(seed: int = 0) -> dict:
    key = jax.random.key(seed)
    ks = jax.random.split(key, 14)
    nrm = jax.random.normal
    f32 = jnp.float32
    return {
        "x": nrm(ks[0], (BATCH, SEQ, D_MODEL), f32),
        "norm_mix_g": 1.0 + 0.02 * nrm(ks[1], (DEPTH, D_MODEL), f32),
        "norm_ffn_g": 1.0 + 0.02 * nrm(ks[2], (DEPTH, D_MODEL), f32),
        "norm_final_g": 1.0 + 0.02 * nrm(ks[3], (D_MODEL,), f32),
        "w_in": nrm(ks[4], (DEPTH, D_MODEL, P_IN), f32) * D_MODEL ** -0.5,
        "fox_b_f": 1.0 + 0.1 * nrm(ks[5], (DEPTH, N_FOX), f32),
        "t5_bias": 0.5 * nrm(ks[6], (T5_BUCKETS, N_MOBA + N_DSA), f32),
        "w_gate": nrm(ks[7], (DEPTH, N_BRANCH, D_MODEL, D_MODEL), f32) * D_MODEL ** -0.5,
        "w_branch": nrm(ks[8], (DEPTH, N_BRANCH, BRANCH_W, D_MODEL), f32) * BRANCH_W ** -0.5,
        "w_out": nrm(ks[9], (DEPTH, D_MODEL, D_MODEL), f32) * D_MODEL ** -0.5,
        "w_up": nrm(ks[10], (DEPTH, D_MODEL, 2 * D_FF), f32) * D_MODEL ** -0.5,
        "conv_w": nrm(ks[11], (DEPTH, CONV_W, 2 * D_FF), f32) * CONV_W ** -0.5,
        "conv_b": 0.01 * nrm(ks[12], (DEPTH, 2 * D_FF), f32),
        "w_down": nrm(ks[13], (DEPTH, D_FF, D_MODEL), f32) * D_FF ** -0.5,
    }


def reference(x, norm_mix_g, norm_ffn_g, norm_final_g, w_in, fox_b_f, t5_bias,
              w_gate, w_branch, w_out, w_up, conv_w, conv_b, w_down):
    for l in range(DEPTH):
        x = x + token_mixer(rmsnorm(x, norm_mix_g[l]), w_in[l], fox_b_f[l], t5_bias,
                            w_gate[l], w_branch[l], w_out[l])
        x = x + conv_ffn(rmsnorm(x, norm_ffn_g[l]), w_up[l], conv_w[l], conv_b[l], w_down[l])
    return rmsnorm(x, norm_final_g)
```

```python
import functools
import math

import jax
import jax.numpy as jnp
from jax import lax
from jax.experimental import pallas as pl
from jax.experimental.pallas import tpu as pltpu

F32 = jnp.float32
BF16 = jnp.bfloat16
I32 = jnp.int32

HEAD_DIM = 64
N_HEADS = 4
BRANCH_W = N_HEADS * HEAD_DIM
N_BRANCH = 4
MOBA_BLOCK = 256
MOBA_TOPK = 3
DSA_TOPK = 256
IDX_HEADS = 8
IDX_DIM = 32
T5_BUCKETS = 32
T5_MAX_DIST = 128
CONV_W = 3
RMS_EPS = 1e-6
NEG_BIG = -1e30
INT_MIN = -2147483648

TQ = 256
TK = 128
Q_PER_K = TQ // TK
T5_CONST_DIST = 113
N_DIAG = (TQ + T5_CONST_DIST - 2) // TK + 1
FAR_DIST = 1 << 20

KIND_CAUSAL = 8
KIND_STRICT_ADD = 10
KIND_STRICT_MUL = 11
N_KINDS = 12

VMEM_LIMIT = 56 * 1024 * 1024


def _dot(a, b):
    return jnp.dot(a, b, preferred_element_type=F32)


def _dot_nt(a, b):
    return lax.dot_general(a, b, (((1,), (1,)), ((), ())), preferred_element_type=F32)


def _rmsnorm(x, g):
    return x * lax.rsqrt(jnp.mean(x * x, axis=-1, keepdims=True) + RMS_EPS) * g


def _softplus(z):
    return jnp.maximum(z, 0.0) + jnp.log1p(jnp.exp(-jnp.abs(z)))


def _params(*sem):
    return pltpu.CompilerParams(dimension_semantics=sem, vmem_limit_bytes=VMEM_LIMIT)


def _tiles_kernel(tab_ref, out_ref):
    kind = pl.program_id(0)
    d = pl.program_id(1)
    sl = lax.broadcasted_iota(I32, (TK, TQ), 0)
    tl = lax.broadcasted_iota(I32, (TK, TQ), 1)
    dist = jnp.where(d == N_DIAG, FAR_DIST, (TK - TQ) + d * TK + tl - sl)
    n = jnp.maximum(dist, 0)
    max_exact = T5_BUCKETS // 2
    nf = jnp.maximum(n, 1).astype(F32)
    large = max_exact + (jnp.log(nf / max_exact) / math.log(T5_MAX_DIST / max_exact)
                         * (T5_BUCKETS - max_exact)).astype(I32)
    large = jnp.minimum(large, T5_BUCKETS - 1)
    bucket = jnp.where(n < max_exact, n, large)
    h = jnp.minimum(kind, 2 * N_HEADS - 1)
    bias = jnp.zeros((TK, TQ), F32)
    for b in range(T5_BUCKETS):
        bias = jnp.where(bucket == b, tab_ref[b, h], bias)
    causal = dist >= 0
    strict = dist > 0
    v_bias = jnp.where(causal, bias, NEG_BIG)
    v_causal = jnp.where(causal, 0.0, NEG_BIG)
    v_sadd = jnp.where(strict, 0.0, NEG_BIG)
    v_smul = jnp.where(strict, 1.0, 0.0)
    out_ref[0, 0] = jnp.where(kind < KIND_CAUSAL, v_bias,
                              jnp.where(kind < KIND_STRICT_ADD, v_causal,
                                        jnp.where(kind == KIND_STRICT_ADD, v_sadd, v_smul)))


def _make_tiles(t5_bias):
    return pl.pallas_call(
        _tiles_kernel,
        grid=(N_KINDS, N_DIAG + 1),
        in_specs=[pl.BlockSpec(memory_space=pltpu.SMEM)],
        out_specs=pl.BlockSpec((1, 1, TK, TQ), lambda k, d: (k, d, 0, 0)),
        out_shape=jax.ShapeDtypeStruct((N_KINDS, N_DIAG + 1, TK, TQ), F32),
        compiler_params=_params("arbitrary", "arbitrary"),
        name="t5_tiles",
    )(t5_bias)


N_MAIN = 8 * BRANCH_W
N_NAT = N_MAIN + 256
N_VT = 3 * BRANCH_W + HEAD_DIM
N_T = N_VT + 16
PROJ_CHUNK = 512


def _proj_kernel(x_ref, g_ref, w_ref, wt_ref, main_ref, kv_ref, fxf_ref, vt_ref, iw_ref):
    xn = _rmsnorm(x_ref[0], g_ref[...]).astype(BF16)
    for c in range(0, N_MAIN, PROJ_CHUNK):
        main_ref[0, :, c:c + PROJ_CHUNK] = _dot(xn, w_ref[:, c:c + PROJ_CHUNK]).astype(BF16)
    r = _dot(xn, w_ref[:, N_MAIN:N_NAT])
    kv_ref[0] = r[:, :128].astype(BF16)
    fxf_ref[0] = r[:, 128:]
    t = _dot_nt(wt_ref[...], xn)
    vt_ref[0] = t[:N_VT].astype(BF16)
    iw_ref[0] = t[N_VT:]


def _proj(x, g, w_nat, w_t):
    b, s, d = x.shape
    tm = min(512, s)
    return pl.pallas_call(
        _proj_kernel,
        grid=(b, s // tm),
        in_specs=[
            pl.BlockSpec((1, tm, d), lambda bi, i: (bi, i, 0)),
            pl.BlockSpec((1, d), lambda bi, i: (0, 0)),
            pl.BlockSpec((d, N_NAT), lambda bi, i: (0, 0)),
            pl.BlockSpec((N_T, d), lambda bi, i: (0, 0)),
        ],
        out_specs=[
            pl.BlockSpec((1, tm, N_MAIN), lambda bi, i: (bi, i, 0)),
            pl.BlockSpec((1, tm, 128), lambda bi, i: (bi, i, 0)),
            pl.BlockSpec((1, tm, 128), lambda bi, i: (bi, i, 0)),
            pl.BlockSpec((1, N_VT, tm), lambda bi, i: (bi, 0, i)),
            pl.BlockSpec((1, 16, tm), lambda bi, i: (bi, 0, i)),
        ],
        out_shape=[
            jax.ShapeDtypeStruct((b, s, N_MAIN), BF16),
            jax.ShapeDtypeStruct((b, s, 128), BF16),
            jax.ShapeDtypeStruct((b, s, 128), F32),
            jax.ShapeDtypeStruct((b, N_VT, s), BF16),
            jax.ShapeDtypeStruct((b, 16, s), F32),
        ],
        compiler_params=_params("arbitrary", "arbitrary"),
        name="norm_in_proj",
    )(x, g.reshape(1, d), w_nat, w_t)


def _pack_in_weights(w_in):
    d = w_in.shape[0]
    sizes = ([BRANCH_W] * 9 + [N_HEADS] + [BRANCH_W, HEAD_DIM, HEAD_DIM]
             + [IDX_HEADS * IDX_DIM, IDX_DIM, IDX_HEADS])
    offs = [0]
    for sz in sizes:
        offs.append(offs[-1] + sz)
    seg = [w_in[:, offs[i]:offs[i + 1]] for i in range(len(sizes))]
    (sb_q, sb_k, sb_v, mb_q, mb_k, mb_v, fx_q, fx_k, fx_v, fx_f,
     ds_q, ds_k, ds_v, ix_q, ix_k, ix_w) = seg
    scale = HEAD_DIM ** -0.5
    z = lambda n: jnp.zeros((d, n), w_in.dtype)
    w_nat = jnp.concatenate(
        [sb_q * scale, sb_k, mb_q * scale, mb_k, fx_q * scale, fx_k, ds_q * scale, ix_q,
         ds_k, ix_k, z(128 - HEAD_DIM - IDX_DIM),
         fx_f, fx_f, fx_f, z(128 - 3 * N_HEADS)], axis=1).astype(BF16)
    w_t = jnp.concatenate([sb_v, mb_v, fx_v, ds_v, ix_w, z(16 - IDX_HEADS)], axis=1).T.astype(BF16)
    return w_nat, w_t


CUM_BLOCK = 256


def _fox_c_kernel(f_ref, b_ref, p_ref):
    s = f_ref.shape[1]
    row = lax.broadcasted_iota(I32, (CUM_BLOCK, CUM_BLOCK), 0)
    col = lax.broadcasted_iota(I32, (CUM_BLOCK, CUM_BLOCK), 1)
    tri = jnp.where(row >= col, 1.0, 0.0).astype(BF16)
    tri3 = jnp.concatenate([tri, tri, tri], axis=1)
    lane = lax.broadcasted_iota(I32, (CUM_BLOCK, 128), 1)
    carry = jnp.zeros((1, 128), F32)
    for blk in range(s // CUM_BLOCK):
        f = f_ref[0, blk * CUM_BLOCK:(blk + 1) * CUM_BLOCK, :] + b_ref[...]
        ls = -_softplus(-f)
        hi = ls.astype(BF16)
        r1 = ls - hi.astype(F32)
        mid = r1.astype(BF16)
        lo = (r1 - mid.astype(F32)).astype(BF16)
        c = _dot(tri3, jnp.concatenate([hi, mid, lo], axis=0)) + carry
        carry = c[CUM_BLOCK - 1:CUM_BLOCK, :]
        c_hi = c.astype(BF16).astype(F32)
        r2 = c - c_hi
        c_mid = r2.astype(BF16).astype(F32)
        c_lo = r2 - c_mid
        piece = jnp.where(lane < N_HEADS, c_hi, jnp.where(lane < 2 * N_HEADS, c_mid, c_lo))
        p_ref[0, blk * CUM_BLOCK:(blk + 1) * CUM_BLOCK, :] = piece.astype(BF16)


def _fox_c(fxf, fox_b):
    b, s, _ = fxf.shape
    bias = jnp.concatenate([fox_b, fox_b, fox_b, jnp.zeros((128 - 3 * N_HEADS,), F32)]).reshape(1, 128)
    return pl.pallas_call(
        _fox_c_kernel,
        grid=(b,),
        in_specs=[pl.BlockSpec((1, s, 128), lambda bi: (bi, 0, 0)),
                  pl.BlockSpec((1, 128), lambda bi: (0, 0))],
        out_specs=pl.BlockSpec((1, s, 128), lambda bi: (bi, 0, 0)),
        out_shape=jax.ShapeDtypeStruct((b, s, 128), BF16),
        compiler_params=_params("arbitrary"),
        name="fox_cumgate",
    )(fxf, bias)


def _flash_update(h, z, vt_j, acc_ref, m_ref, l_ref):
    rows = slice(h * HEAD_DIM, (h + 1) * HEAD_DIM)
    m_old = m_ref[h:h + 1, :]
    m_new = jnp.maximum(m_old, jnp.max(z, axis=0, keepdims=True))
    alpha = jnp.exp(m_old - m_new)
    p = jnp.exp(z - m_new)
    l_ref[h:h + 1, :] = alpha * l_ref[h:h + 1, :] + jnp.sum(p, axis=0, keepdims=True)
    acc_ref[rows, :] = alpha * acc_ref[rows, :] + _dot(vt_j, p.astype(BF16))
    m_ref[h:h + 1, :] = m_new


def _flash_init(acc_ref, m_ref, l_ref):
    acc_ref[...] = jnp.zeros_like(acc_ref)
    m_ref[...] = jnp.full_like(m_ref, -1e38)
    l_ref[...] = jnp.zeros_like(l_ref)


def _flash_store(o_ref, acc_ref, l_ref):
    for h in range(N_HEADS):
        rows = slice(h * HEAD_DIM, (h + 1) * HEAD_DIM)
        acc_ref[rows, :] = acc_ref[rows, :] * (1.0 / l_ref[h:h + 1, :])
    o_ref[0] = acc_ref[...].T.astype(o_ref.dtype)


def _sb_kernel(q_ref, k_ref, vt_ref, tb_ref, o_ref, acc_ref, r_ref):
    i = pl.program_id(1)
    jmax = (i + 1) * Q_PER_K - 1
    row = lax.broadcasted_iota(I32, (TK, 2 * TK), 0)
    col = lax.broadcasted_iota(I32, (TK, 2 * TK), 1)
    upper2 = jnp.where((col & (TK - 1)) > row, 1.0, 0.0).astype(BF16)
    acc_ref[...] = jnp.zeros_like(acc_ref)
    r_ref[...] = jnp.zeros_like(r_ref)

    def body(jj, carry):
        j = jmax - jj
        d = jnp.minimum(jj, N_DIAG)
        s0 = pl.multiple_of(j * TK, TK)
        sadd = tb_ref[0, d]
        smul = tb_ref[1, d]
        for h in range(N_HEADS):
            cols = slice(h * HEAD_DIM, (h + 1) * HEAD_DIM)
            z = _dot_nt(k_ref[0, pl.ds(s0, TK), cols], q_ref[0, :, cols])
            sp = _softplus(z)
            lneg = -sp * smul
            hi = lneg.astype(BF16)
            lo = (lneg - hi.astype(F32)).astype(BF16)
            excl = _dot(upper2, jnp.concatenate([hi, lo], axis=0)) + r_ref[h:h + 1, :]
            w = jnp.exp(z - sp + excl + sadd)
            acc_ref[cols, :] += _dot(vt_ref[0, cols, pl.ds(s0, TK)], w.astype(BF16))
            r_ref[h:h + 1, :] += jnp.sum(lneg, axis=0, keepdims=True)
        return carry

    lax.fori_loop(0, jmax + 1, body, 0)
    o_ref[0] = acc_ref[...].T.astype(o_ref.dtype)


def _moba_kernel(q_ref, k_ref, vt_ref, tb_ref, o_ref, acc_ref, m_ref, l_ref, selb_ref):
    i = pl.program_id(1)
    own = i
    jmax = (i + 1) * Q_PER_K - 1
    nb = k_ref.shape[1] // MOBA_BLOCK
    _flash_init(acc_ref, m_ref, l_ref)

    for h in range(N_HEADS):
        cols = slice(h * HEAD_DIM, (h + 1) * HEAD_DIM)
        q_h = q_ref[0, :, cols]
        kms = [jnp.mean(k_ref[0, n * MOBA_BLOCK:(n + 1) * MOBA_BLOCK, cols].astype(F32),
                        axis=0, keepdims=True) for n in range(nb)]
        kms += [jnp.zeros((1, HEAD_DIM), F32)] * (8 - nb)
        km = jnp.concatenate(kms, axis=0)
        km_hi = km.astype(BF16)
        km_lo = (km - km_hi.astype(F32)).astype(BF16)
        gate = _dot_nt(km_hi, q_h) + _dot_nt(km_lo, q_h)
        g = [gate[n:n + 1, :] for n in range(nb)]
        rows = []
        for n in range(nb):
            rank = jnp.zeros((1, TQ), F32)
            for m in range(nb):
                if m == n:
                    continue
                beats = (g[m] >= g[n]) if m < n else (g[m] > g[n])
                rank = rank + jnp.where(beats, jnp.where(m < own, 1.0, 0.0), 0.0)
            chosen = jnp.where(rank < MOBA_TOPK, 0.0, NEG_BIG)
            rows.append(jnp.where(n < own, chosen, jnp.where(n == own, 0.0, NEG_BIG)))
        rows += [jnp.full((1, TQ), NEG_BIG, F32)] * (8 - nb)
        selb_ref[h] = jnp.concatenate(rows, axis=0)

    def body(jj, carry):
        j = jmax - jj
        d = jnp.minimum(jj, N_DIAG)
        n = j // (MOBA_BLOCK // TK)
        s0 = pl.multiple_of(j * TK, TK)
        for h in range(N_HEADS):
            cols = slice(h * HEAD_DIM, (h + 1) * HEAD_DIM)
            z = _dot_nt(k_ref[0, pl.ds(s0, TK), cols], q_ref[0, :, cols])
            z = z + tb_ref[h, d] + selb_ref[h, pl.ds(n, 1), :]
            _flash_update(h, z, vt_ref[0, cols, pl.ds(s0, TK)], acc_ref, m_ref, l_ref)
        return carry

    lax.fori_loop(0, jmax + 1, body, 0)
    _flash_store(o_ref, acc_ref, l_ref)


def _fox_kernel(q_ref, k_ref, vt_ref, tb_ref, o_ref, acc_ref, m_ref, l_ref):
    i = pl.program_id(1)
    jmax = (i + 1) * Q_PER_K - 1
    _flash_init(acc_ref, m_ref, l_ref)

    def body(jj, carry):
        j = jmax - jj
        d = jnp.minimum(jj, N_DIAG)
        s0 = pl.multiple_of(j * TK, TK)
        causal = tb_ref[0, d]
        for h in range(N_HEADS):
            aug = slice(h * 128, (h + 1) * 128)
            cols = slice(h * HEAD_DIM, (h + 1) * HEAD_DIM)
            z = _dot_nt(k_ref[0, pl.ds(s0, TK), aug], q_ref[0, :, aug]) + causal
            _flash_update(h, z, vt_ref[0, cols, pl.ds(s0, TK)], acc_ref, m_ref, l_ref)
        return carry

    lax.fori_loop(0, jmax + 1, body, 0)
    _flash_store(o_ref, acc_ref, l_ref)


def _dsa_kernel(q_ref, iq_ref, kv_ref, vt_ref, iw_ref, tb_ref, tc_ref, o_ref,
                acc_ref, m_ref, l_ref, keys_ref, *, topk):
    i = pl.program_id(1)
    jmax = (i + 1) * Q_PER_K - 1
    _flash_init(acc_ref, m_ref, l_ref)

    def score_body(j, carry):
        d = jnp.minimum(jmax - j, N_DIAG)
        s0 = pl.multiple_of(j * TK, TK)
        k_idx = kv_ref[0, pl.ds(s0, TK), HEAD_DIM:HEAD_DIM + IDX_DIM]
        sc = jnp.zeros((TK, TQ), F32)
        for hh in range(IDX_HEADS):
            qi = iq_ref[0, :, hh * IDX_DIM:(hh + 1) * IDX_DIM]
            sc = sc + iw_ref[0, hh:hh + 1, :] * jnp.maximum(_dot_nt(k_idx, qi), 0.0)
        sc = jnp.where(sc == 0.0, 0.0, sc)
        bits = lax.bitcast_convert_type(sc, I32)
        key = bits ^ ((bits >> 31) & 0x7FFFFFFF)
        keys_ref[j] = jnp.where(tc_ref[0, d] < 0.0, INT_MIN, key)
        return carry

    lax.fori_loop(0, jmax + 1, score_body, 0)

    def bit_body(it, t_u):
        cand = t_u | (jnp.int32(1) << (31 - it))
        thr_c = cand ^ INT_MIN

        def count_body(j, cnt):
            ge = jnp.where(keys_ref[j] >= thr_c, 1, 0)
            return cnt + jnp.sum(ge.reshape(TK // 8, 8, TQ), axis=0)

        cnt = lax.fori_loop(0, jmax + 1, count_body, jnp.zeros((8, TQ), I32))
        cnt = jnp.sum(cnt, axis=0, keepdims=True)
        return jnp.where(cnt >= topk, cand, t_u)

    thr = lax.fori_loop(0, 32, bit_body, jnp.zeros((1, TQ), I32)) ^ INT_MIN

    def body(jj, carry):
        j = jmax - jj
        d = jnp.minimum(jj, N_DIAG)
        s0 = pl.multiple_of(j * TK, TK)
        k_j = kv_ref[0, pl.ds(s0, TK), 0:HEAD_DIM]
        vt_j = vt_ref[0, :, pl.ds(s0, TK)]
        selb = jnp.where(keys_ref[j] >= thr, 0.0, NEG_BIG)
        for h in range(N_HEADS):
            cols = slice(h * HEAD_DIM, (h + 1) * HEAD_DIM)
            z = _dot_nt(k_j, q_ref[0, :, cols]) + tb_ref[h, d] + selb
            _flash_update(h, z, vt_j, acc_ref, m_ref, l_ref)
        return carry

    lax.fori_loop(0, jmax + 1, body, 0)
    _flash_store(o_ref, acc_ref, l_ref)


def _mixer_call(kernel_fn, name, b, s, in_specs, scratch, args):
    return pl.pallas_call(
        kernel_fn,
        grid=(b, s // TQ),
        in_specs=in_specs,
        out_specs=pl.BlockSpec((1, TQ, BRANCH_W), lambda bi, i: (bi, i, 0)),
        out_shape=jax.ShapeDtypeStruct((b, s, BRANCH_W), BF16),
        scratch_shapes=scratch,
        compiler_params=_params("arbitrary", "arbitrary"),
        name=name,
    )(*args)


def _q_spec(width, col):
    return pl.BlockSpec((1, TQ, width), lambda bi, i: (bi, i, col))


def _k_spec(s, width, col):
    return pl.BlockSpec((1, s, width), lambda bi, i: (bi, 0, col))


def _vt_spec(s, rows, row_blk):
    return pl.BlockSpec((1, rows, s), lambda bi, i: (bi, row_blk, 0))


def _tile_spec(n, blk):
    return pl.BlockSpec((n, N_DIAG + 1, TK, TQ), lambda bi, i: (blk, 0, 0, 0))


_ACC = lambda: pltpu.VMEM((BRANCH_W, TQ), F32)
_ROWS = lambda: pltpu.VMEM((8, TQ), F32)


def _sb(main, vt, tiles):
    b, s, _ = main.shape
    return _mixer_call(
        _sb_kernel, "sb_mixer", b, s,
        [_q_spec(BRANCH_W, 0), _k_spec(s, BRANCH_W, 1), _vt_spec(s, BRANCH_W, 0),
         _tile_spec(2, KIND_STRICT_ADD // 2)],
        [_ACC(), _ROWS()], (main, main, vt, tiles))


def _moba(main, vt, tiles):
    b, s, _ = main.shape
    return _mixer_call(
        _moba_kernel, "moba_mixer", b, s,
        [_q_spec(BRANCH_W, 2), _k_spec(s, BRANCH_W, 3), _vt_spec(s, BRANCH_W, 1),
         _tile_spec(N_HEADS, 0)],
        [_ACC(), _ROWS(), _ROWS(), pltpu.VMEM((N_HEADS, 8, TQ), F32)], (main, main, vt, tiles))


def _fox(q_aug, k_aug, vt, tiles):
    b, s, _ = q_aug.shape
    return _mixer_call(
        _fox_kernel, "fox_mixer", b, s,
        [_q_spec(N_HEADS * 128, 0), _k_spec(s, N_HEADS * 128, 0), _vt_spec(s, BRANCH_W, 2),
         _tile_spec(1, KIND_CAUSAL)],
        [_ACC(), _ROWS(), _ROWS()], (q_aug, k_aug, vt, tiles))


def _dsa(main, kv, vt, iw, tiles):
    b, s, _ = main.shape
    topk = min(DSA_TOPK, s // 4)
    return _mixer_call(
        functools.partial(_dsa_kernel, topk=topk), "dsa_mixer", b, s,
        [_q_spec(BRANCH_W, 6), _q_spec(BRANCH_W, 7), _k_spec(s, 128, 0),
         _vt_spec(s, HEAD_DIM, 3 * BRANCH_W // HEAD_DIM),
         pl.BlockSpec((1, 16, TQ), lambda bi, i: (bi, 0, i)),
         _tile_spec(N_HEADS, 1), _tile_spec(1, KIND_CAUSAL)],
        [_ACC(), _ROWS(), _ROWS(), pltpu.VMEM((s // TK, TK, TQ), I32)],
        (main, main, kv, vt, iw, tiles, tiles))


def _fox_augment(main, pieces):
    b, s, _ = main.shape
    q = main[:, :, 4 * BRANCH_W:5 * BRANCH_W].reshape(b, s, N_HEADS, HEAD_DIM)
    k = main[:, :, 5 * BRANCH_W:6 * BRANCH_W].reshape(b, s, N_HEADS, HEAD_DIM)
    c = pieces[:, :, :3 * N_HEADS].reshape(b, s, 3, N_HEADS).transpose(0, 1, 3, 2)
    one = jnp.ones((b, s, N_HEADS, 3), BF16)
    pad = jnp.zeros((b, s, N_HEADS, 128 - HEAD_DIM - 6), BF16)
    q_aug = jnp.concatenate([q, -one, c, pad], axis=-1).reshape(b, s, N_HEADS * 128)
    k_aug = jnp.concatenate([k, c, one, pad], axis=-1).reshape(b, s, N_HEADS * 128)
    return q_aug, k_aug


def _combine_kernel(x_ref, g_ref, o0, o1, o2, o3, wg_ref, wb_ref, wo_ref, out_ref):
    x = x_ref[...]
    xn = _rmsnorm(x, g_ref[...]).astype(BF16)
    mixed = None
    for i, o_ref in enumerate((o0, o1, o2, o3)):
        gate = 1.0 / (1.0 + jnp.exp(-_dot(xn, wg_ref[i])))
        y = gate * _dot(o_ref[...], wb_ref[i])
        mixed = y if mixed is None else mixed + y
    out_ref[...] = x + _dot(mixed.astype(BF16), wo_ref[...])


def _combine(x2, g, outs, wg, wb, wo):
    t, d = x2.shape
    tm = 256
    row = lambda i: (i, 0)
    return pl.pallas_call(
        _combine_kernel,
        grid=(t // tm,),
        in_specs=[pl.BlockSpec((tm, d), row), pl.BlockSpec((1, d), lambda i: (0, 0))]
        + [pl.BlockSpec((tm, BRANCH_W), row)] * N_BRANCH
        + [pl.BlockSpec((N_BRANCH, d, d), lambda i: (0, 0, 0)),
           pl.BlockSpec((N_BRANCH, BRANCH_W, d), lambda i: (0, 0, 0)),
           pl.BlockSpec((d, d), lambda i: (0, 0))],
        out_specs=pl.BlockSpec((tm, d), row),
        out_shape=jax.ShapeDtypeStruct((t, d), F32),
        compiler_params=_params("arbitrary"),
        name="gate_combine_out",
    )(x2, g.reshape(1, d), *outs, wg, wb, wo)


FFN_CHUNK = 256
HALO = 16


def _gelu(g):
    return 0.5 * g * (1.0 + lax.erf(g * (2.0 ** -0.5)))


def _ffn_kernel(x_ref, xp_ref, g_ref, wg_ref, wu_ref, cw_ref, wd_ref, o_ref, xn_s, hg_s, hu_s, acc_s):
    i = pl.program_id(1)
    tm = x_ref.shape[1]
    x = x_ref[0]
    gain = g_ref[...]
    prev = _rmsnorm(xp_ref[0], gain) * jnp.where(i > 0, 1.0, 0.0)
    xn_s[...] = jnp.concatenate([jnp.zeros_like(prev), prev, _rmsnorm(x, gain)], axis=0).astype(BF16)
    acc_s[...] = jnp.zeros_like(acc_s)

    def taps(h_s, cw, r):
        out = cw[r + 3:r + 4, :]
        for j in range(CONV_W):
            out = out + cw[r + j:r + j + 1, :] * h_s[HALO - (CONV_W - 1) + j:HALO - (CONV_W - 1) + j + tm, :]
        return out

    def body(c, carry):
        xn = xn_s[...]
        hg_s[...] = _dot(xn, wg_ref[c])
        hu_s[...] = _dot(xn, wu_ref[c])
        cw = cw_ref[c]
        a = _gelu(taps(hg_s, cw, 0)) * taps(hu_s, cw, 4)
        acc_s[...] += _dot(a.astype(BF16), wd_ref[c])
        return carry

    lax.fori_loop(0, wg_ref.shape[0], body, 0)
    o_ref[0] = x + acc_s[...]


def _ffn(x, g, wg, wu, cw, wd):
    b, s, d = x.shape
    tm = 256
    nc = wg.shape[0]
    const3 = lambda bi, i: (0, 0, 0)
    return pl.pallas_call(
        _ffn_kernel,
        grid=(b, s // tm),
        in_specs=[
            pl.BlockSpec((1, tm, d), lambda bi, i: (bi, i, 0)),
            pl.BlockSpec((1, 8, d), lambda bi, i: (bi, jnp.maximum(i * (tm // 8) - 1, 0), 0)),
            pl.BlockSpec((1, d), lambda bi, i: (0, 0)),
            pl.BlockSpec((nc, d, FFN_CHUNK), const3),
            pl.BlockSpec((nc, d, FFN_CHUNK), const3),
            pl.BlockSpec((nc, 8, FFN_CHUNK), const3),
            pl.BlockSpec((nc, FFN_CHUNK, d), const3),
        ],
        out_specs=pl.BlockSpec((1, tm, d), lambda bi, i: (bi, i, 0)),
        out_shape=jax.ShapeDtypeStruct((b, s, d), F32),
        scratch_shapes=[pltpu.VMEM((tm + HALO, d), BF16),
                        pltpu.VMEM((tm + HALO, FFN_CHUNK), F32),
                        pltpu.VMEM((tm + HALO, FFN_CHUNK), F32),
                        pltpu.VMEM((tm, d), F32)],
        compiler_params=_params("arbitrary", "arbitrary"),
        name="conv_ffn",
    )(x, x, g.reshape(1, d), wg, wu, cw, wd)


def _pack_ffn_weights(w_up, conv_w, conv_b, w_down):
    d, two_f = w_up.shape
    f = two_f // 2
    nc = f // FFN_CHUNK
    chunks = lambda w: w.reshape(w.shape[0], nc, FFN_CHUNK).transpose(1, 0, 2)
    wg = chunks(w_up[:, :f]).astype(BF16)
    wu = chunks(w_up[:, f:]).astype(BF16)
    cw = jnp.concatenate([conv_w[:, :f], conv_b[None, :f], conv_w[:, f:], conv_b[None, f:]], axis=0)
    wd = w_down.reshape(nc, FFN_CHUNK, d).astype(BF16)
    return wg, wu, chunks(cw), wd


def _norm_kernel(x_ref, g_ref, o_ref):
    o_ref[...] = _rmsnorm(x_ref[...], g_ref[...])


def _final_norm(x2, g):
    t, d = x2.shape
    tm = 512
    return pl.pallas_call(
        _norm_kernel,
        grid=(t // tm,),
        in_specs=[pl.BlockSpec((tm, d), lambda i: (i, 0)), pl.BlockSpec((1, d), lambda i: (0, 0))],
        out_specs=pl.BlockSpec((tm, d), lambda i: (i, 0)),
        out_shape=jax.ShapeDtypeStruct((t, d), F32),
        compiler_params=_params("arbitrary"),
        name="final_norm",
    )(x2, g.reshape(1, d))


def _token_mixers(x, g, w_in, fox_b, tiles):
    w_nat, w_t = _pack_in_weights(w_in)
    main, kv, fxf, vt, iw = _proj(x, g, w_nat, w_t)
    pieces = _fox_c(fxf, fox_b)
    q_aug, k_aug = _fox_augment(main, pieces)
    return (_sb(main, vt, tiles), _moba(main, vt, tiles), _fox(q_aug, k_aug, vt, tiles),
            _dsa(main, kv, vt, iw, tiles))


def kernel(x, norm_mix_g, norm_ffn_g, norm_final_g, w_in, fox_b_f, t5_bias, w_gate, w_branch, w_out,
           w_up, conv_w, conv_b, w_down):
    b, s, d = x.shape
    assert s % TQ == 0 and TQ == MOBA_BLOCK and s // MOBA_BLOCK <= 8
    tiles = _make_tiles(t5_bias)
    for l in range(w_in.shape[0]):
        outs = _token_mixers(x, norm_mix_g[l], w_in[l], fox_b_f[l], tiles)
        outs = [o.reshape(b * s, BRANCH_W) for o in outs]
        x = _combine(x.reshape(b * s, d), norm_mix_g[l], outs, w_gate[l].astype(BF16),
                     w_branch[l].astype(BF16), w_out[l].astype(BF16)).reshape(b, s, d)
        x = _ffn(x, norm_ffn_g[l], *_pack_ffn_weights(w_up[l], conv_w[l], conv_b[l], w_down[l]))
    return _final_norm(x.reshape(b * s, d), norm_final_g).reshape(b, s, d)
```

```python
import functools
import math

import jax
import jax.numpy as jnp
from jax import lax
from jax.experimental import pallas as pl
from jax.experimental.pallas import tpu as pltpu

F32 = jnp.float32
BF16 = jnp.bfloat16
I32 = jnp.int32

HEAD_DIM = 64
N_HEADS = 4
BRANCH_W = N_HEADS * HEAD_DIM
N_BRANCH = 4
MOBA_BLOCK = 256
MOBA_TOPK = 3
DSA_TOPK = 256
IDX_HEADS = 8
IDX_DIM = 32
T5_BUCKETS = 32
T5_MAX_DIST = 128
CONV_W = 3
RMS_EPS = 1e-6
NEG_BIG = -1e30
INT_MIN = -2147483648

TQ = 256
TK = 128
Q_PER_K = TQ // TK
T5_CONST_DIST = 113
N_DIAG = (TQ + T5_CONST_DIST - 2) // TK + 1
FAR_DIST = 1 << 20

KIND_CAUSAL = 8
KIND_STRICT_ADD = 10
KIND_STRICT_MUL = 11
N_KINDS = 12

VMEM_LIMIT = 56 * 1024 * 1024


def _dot(a, b):
    return jnp.dot(a, b, preferred_element_type=F32)


def _dot_nt(a, b):
    return lax.dot_general(a, b, (((1,), (1,)), ((), ())), preferred_element_type=F32)


def _rmsnorm(x, g):
    return x * lax.rsqrt(jnp.mean(x * x, axis=-1, keepdims=True) + RMS_EPS) * g


def _softplus(z):
    return jnp.maximum(z, 0.0) + jnp.log1p(jnp.exp(-jnp.abs(z)))


def _params(*sem):
    return pltpu.CompilerParams(dimension_semantics=sem, vmem_limit_bytes=VMEM_LIMIT)


def _tiles_kernel(tab_ref, out_ref):
    kind = pl.program_id(0)
    d = pl.program_id(1)
    sl = lax.broadcasted_iota(I32, (TK, TQ), 0)
    tl = lax.broadcasted_iota(I32, (TK, TQ), 1)
    dist = jnp.where(d == N_DIAG, FAR_DIST, (TK - TQ) + d * TK + tl - sl)
    n = jnp.maximum(dist, 0)
    max_exact = T5_BUCKETS // 2
    nf = jnp.maximum(n, 1).astype(F32)
    large = max_exact + (jnp.log(nf / max_exact) / math.log(T5_MAX_DIST / max_exact)
                         * (T5_BUCKETS - max_exact)).astype(I32)
    large = jnp.minimum(large, T5_BUCKETS - 1)
    bucket = jnp.where(n < max_exact, n, large)
    h = jnp.minimum(kind, 2 * N_HEADS - 1)
    bias = jnp.zeros((TK, TQ), F32)
    for b in range(T5_BUCKETS):
        bias = jnp.where(bucket == b, tab_ref[b, h], bias)
    causal = dist >= 0
    strict = dist > 0
    v_bias = jnp.where(causal, bias, NEG_BIG)
    v_causal = jnp.where(causal, 0.0, NEG_BIG)
    v_sadd = jnp.where(strict, 0.0, NEG_BIG)
    v_smul = jnp.where(strict, -1.0, 0.0)
    out_ref[0, 0] = jnp.where(kind < KIND_CAUSAL, v_bias,
                              jnp.where(kind < KIND_STRICT_ADD, v_causal,
                                        jnp.where(kind == KIND_STRICT_ADD, v_sadd, v_smul)))


def _make_tiles(t5_bias):
    return pl.pallas_call(
        _tiles_kernel,
        grid=(N_KINDS, N_DIAG + 1),
        in_specs=[pl.BlockSpec(memory_space=pltpu.SMEM)],
        out_specs=pl.BlockSpec((1, 1, TK, TQ), lambda k, d: (k, d, 0, 0)),
        out_shape=jax.ShapeDtypeStruct((N_KINDS, N_DIAG + 1, TK, TQ), F32),
        compiler_params=_params("arbitrary", "arbitrary"),
        name="t5_tiles",
    )(t5_bias)


N_MAIN = 8 * BRANCH_W
N_NAT = N_MAIN + 256
N_VT = 3 * BRANCH_W + HEAD_DIM
N_T = N_VT + 16
PROJ_CHUNK = 512


def _proj_kernel(x_ref, g_ref, w_ref, wt_ref, main_ref, kv_ref, fxf_ref, vt_ref, iw_ref):
    xn = _rmsnorm(x_ref[0], g_ref[...]).astype(BF16)
    for c in range(0, N_MAIN, PROJ_CHUNK):
        main_ref[0, :, c:c + PROJ_CHUNK] = _dot(xn, w_ref[:, c:c + PROJ_CHUNK]).astype(BF16)
    r = _dot(xn, w_ref[:, N_MAIN:N_NAT])
    kv_ref[0] = r[:, :128].astype(BF16)
    fxf_ref[0] = r[:, 128:]
    t = _dot_nt(wt_ref[...], xn)
    vt_ref[0] = t[:N_VT].astype(BF16)
    iw_ref[0] = t[N_VT:]


def _proj(x, g, w_nat, w_t):
    b, s, d = x.shape
    tm = min(512, s)
    return pl.pallas_call(
        _proj_kernel,
        grid=(b, s // tm),
        in_specs=[
            pl.BlockSpec((1, tm, d), lambda bi, i: (bi, i, 0)),
            pl.BlockSpec((1, d), lambda bi, i: (0, 0)),
            pl.BlockSpec((d, N_NAT), lambda bi, i: (0, 0)),
            pl.BlockSpec((N_T, d), lambda bi, i: (0, 0)),
        ],
        out_specs=[
            pl.BlockSpec((1, tm, N_MAIN), lambda bi, i: (bi, i, 0)),
            pl.BlockSpec((1, tm, 128), lambda bi, i: (bi, i, 0)),
            pl.BlockSpec((1, tm, 128), lambda bi, i: (bi, i, 0)),
            pl.BlockSpec((1, N_VT, tm), lambda bi, i: (bi, 0, i)),
            pl.BlockSpec((1, 16, tm), lambda bi, i: (bi, 0, i)),
        ],
        out_shape=[
            jax.ShapeDtypeStruct((b, s, N_MAIN), BF16),
            jax.ShapeDtypeStruct((b, s, 128), BF16),
            jax.ShapeDtypeStruct((b, s, 128), F32),
            jax.ShapeDtypeStruct((b, N_VT, s), BF16),
            jax.ShapeDtypeStruct((b, 16, s), F32),
        ],
        compiler_params=_params("arbitrary", "arbitrary"),
        name="norm_in_proj",
    )(x, g.reshape(1, d), w_nat, w_t)


def _pack_in_weights(w_in):
    d = w_in.shape[0]
    sizes = ([BRANCH_W] * 9 + [N_HEADS] + [BRANCH_W, HEAD_DIM, HEAD_DIM]
             + [IDX_HEADS * IDX_DIM, IDX_DIM, IDX_HEADS])
    offs = [0]
    for sz in sizes:
        offs.append(offs[-1] + sz)
    seg = [w_in[:, offs[i]:offs[i + 1]] for i in range(len(sizes))]
    (sb_q, sb_k, sb_v, mb_q, mb_k, mb_v, fx_q, fx_k, fx_v, fx_f,
     ds_q, ds_k, ds_v, ix_q, ix_k, ix_w) = seg
    scale = HEAD_DIM ** -0.5
    z = lambda n: jnp.zeros((d, n), w_in.dtype)
    w_nat = jnp.concatenate(
        [sb_q * scale, sb_k, mb_q * scale, mb_k, fx_q * scale, fx_k, ds_q * scale, ix_q,
         ds_k, ix_k, z(128 - HEAD_DIM - IDX_DIM),
         fx_f, fx_f, fx_f, z(128 - 3 * N_HEADS)], axis=1).astype(BF16)
    w_t = jnp.concatenate([sb_v, mb_v, fx_v, ds_v, ix_w, z(16 - IDX_HEADS)], axis=1).T.astype(BF16)
    return w_nat, w_t


CUM_BLOCK = 256


def _fox_c_kernel(f_ref, b_ref, p_ref):
    s = f_ref.shape[1]
    row = lax.broadcasted_iota(I32, (CUM_BLOCK, CUM_BLOCK), 0)
    col = lax.broadcasted_iota(I32, (CUM_BLOCK, CUM_BLOCK), 1)
    tri = jnp.where(row >= col, 1.0, 0.0).astype(BF16)
    tri3 = jnp.concatenate([tri, tri, tri], axis=1)
    lane = lax.broadcasted_iota(I32, (CUM_BLOCK, 128), 1)
    carry = jnp.zeros((1, 128), F32)
    for blk in range(s // CUM_BLOCK):
        f = f_ref[0, blk * CUM_BLOCK:(blk + 1) * CUM_BLOCK, :] + b_ref[...]
        ls = -_softplus(-f)
        hi = ls.astype(BF16)
        r1 = ls - hi.astype(F32)
        mid = r1.astype(BF16)
        lo = (r1 - mid.astype(F32)).astype(BF16)
        c = _dot(tri3, jnp.concatenate([hi, mid, lo], axis=0)) + carry
        carry = c[CUM_BLOCK - 1:CUM_BLOCK, :]
        c_hi = c.astype(BF16).astype(F32)
        r2 = c - c_hi
        c_mid = r2.astype(BF16).astype(F32)
        c_lo = r2 - c_mid
        piece = jnp.where(lane < N_HEADS, c_hi, jnp.where(lane < 2 * N_HEADS, c_mid, c_lo))
        p_ref[0, blk * CUM_BLOCK:(blk + 1) * CUM_BLOCK, :] = piece.astype(BF16)


def _fox_c(fxf, fox_b):
    b, s, _ = fxf.shape
    bias = jnp.concatenate([fox_b, fox_b, fox_b, jnp.zeros((128 - 3 * N_HEADS,), F32)]).reshape(1, 128)
    return pl.pallas_call(
        _fox_c_kernel,
        grid=(b,),
        in_specs=[pl.BlockSpec((1, s, 128), lambda bi: (bi, 0, 0)),
                  pl.BlockSpec((1, 128), lambda bi: (0, 0))],
        out_specs=pl.BlockSpec((1, s, 128), lambda bi: (bi, 0, 0)),
        out_shape=jax.ShapeDtypeStruct((b, s, 128), BF16),
        compiler_params=_params("arbitrary"),
        name="fox_cumgate",
    )(fxf, bias)


def _tile_pos(j, jmax):
    return jnp.minimum(jmax - j, N_DIAG), pl.multiple_of(j * TK, TK)


def _tile_loop(n_tiles, body, init):
    def trip(t, carry):
        for u in range(Q_PER_K):
            carry = body(t * Q_PER_K + u, carry)
        return carry

    return lax.fori_loop(0, n_tiles // Q_PER_K, trip, init)


def _fold8(x, op):
    return op(x.reshape(TK // 8, 8, TQ), axis=0)


def _neg_rows():
    return tuple(jnp.full((8, TQ), -1e38, F32) for _ in range(N_HEADS))


def _weights_matmul(w_s, vt_ref, o_ref, acc_ref, n_tiles, inv_l, shared_v):
    def zero_body(j, carry):
        s0 = pl.multiple_of(j * TK, TK)
        for h in range(N_HEADS):
            w_s[h, pl.ds(s0, TK), :] = jnp.zeros((TK, TQ), BF16)
        return carry

    lax.fori_loop(n_tiles, w_s.shape[1] // TK, zero_body, 0)
    for h in range(N_HEADS):
        rows = slice(h * HEAD_DIM, (h + 1) * HEAD_DIM)
        out = _dot(vt_ref[0] if shared_v else vt_ref[0, rows, :], w_s[h])
        acc_ref[rows, :] = out if inv_l is None else out * inv_l[h]
    o_ref[0] = acc_ref[...].T.astype(o_ref.dtype)


def _softmax_passes(z_s, p_s, maxes, vt_ref, o_ref, acc_ref, n_tiles, shared_v=False):
    m = [jnp.max(mx, axis=0, keepdims=True) for mx in maxes]

    def exp_body(j, sums):
        s0 = pl.multiple_of(j * TK, TK)
        out = []
        for h in range(N_HEADS):
            p = jnp.exp(z_s[h, pl.ds(s0, TK), :] - m[h])
            p_s[h, pl.ds(s0, TK), :] = p.astype(BF16)
            out.append(sums[h] + _fold8(p, jnp.sum))
        return tuple(out)

    sums = lax.fori_loop(0, n_tiles, exp_body, tuple(jnp.zeros((8, TQ), F32) for _ in range(N_HEADS)))
    inv_l = [1.0 / jnp.sum(sm, axis=0, keepdims=True) for sm in sums]
    _weights_matmul(p_s, vt_ref, o_ref, acc_ref, n_tiles, inv_l, shared_v)


def _sb_kernel(q_ref, k_ref, vt_ref, tb_ref, o_ref, acc_ref, base_s, hl_s, w_s):
    i = pl.program_id(1)
    jmax = (i + 1) * Q_PER_K - 1

    def score_body(j, carry):
        d, s0 = _tile_pos(j, jmax)
        sadd = tb_ref[0, d]
        smul = tb_ref[1, d]
        for h in range(N_HEADS):
            cols = slice(h * HEAD_DIM, (h + 1) * HEAD_DIM)
            z = _dot_nt(k_ref[0, pl.ds(s0, TK), cols], q_ref[0, :, cols])
            sp = jnp.maximum(z, 0.0) + jnp.log(1.0 + jnp.exp(-jnp.abs(z)))
            base_s[h, pl.ds(s0, TK), :] = z - sp + sadd
            lneg = sp * smul
            hi = lneg.astype(BF16)
            lo = (lneg - hi.astype(F32)).astype(BF16)
            hl_s[h, j] = jnp.concatenate([hi, lo], axis=0)
        return carry

    _tile_loop(jmax + 1, score_body, 0)

    row = lax.broadcasted_iota(I32, (TK + 16, 2 * TK), 0)
    col = lax.broadcasted_iota(I32, (TK + 16, 2 * TK), 1)
    upper = jnp.where(((col & (TK - 1)) > row) | (row >= TK), 1.0, 0.0).astype(BF16)

    def weight_body(jj, suffix):
        j = jmax - jj
        s0 = pl.multiple_of(j * TK, TK)
        out = []
        for h in range(N_HEADS):
            r = _dot(upper, hl_s[h, j])
            w = jnp.exp(base_s[h, pl.ds(s0, TK), :] + r[:TK] + suffix[h])
            w_s[h, pl.ds(s0, TK), :] = w.astype(BF16)
            out.append(suffix[h] + r[TK:TK + 1])
        return tuple(out)

    _tile_loop(jmax + 1, weight_body, tuple(jnp.zeros((1, TQ), F32) for _ in range(N_HEADS)))
    _weights_matmul(w_s, vt_ref, o_ref, acc_ref, jmax + 1, None, False)


def _moba_kernel(q_ref, k_ref, vt_ref, tb_ref, o_ref, acc_ref, z_s, p_s, selb_ref):
    i = pl.program_id(1)
    own = i
    jmax = (i + 1) * Q_PER_K - 1
    nb = k_ref.shape[1] // MOBA_BLOCK

    for h in range(N_HEADS):
        cols = slice(h * HEAD_DIM, (h + 1) * HEAD_DIM)
        q_h = q_ref[0, :, cols]
        kms = [jnp.mean(k_ref[0, n * MOBA_BLOCK:(n + 1) * MOBA_BLOCK, cols].astype(F32),
                        axis=0, keepdims=True) for n in range(nb)]
        kms += [jnp.zeros((1, HEAD_DIM), F32)] * (8 - nb)
        km = jnp.concatenate(kms, axis=0)
        km_hi = km.astype(BF16)
        km_lo = (km - km_hi.astype(F32)).astype(BF16)
        gate = _dot_nt(km_hi, q_h) + _dot_nt(km_lo, q_h)
        g = [gate[n:n + 1, :] for n in range(nb)]
        rows = []
        for n in range(nb):
            rank = jnp.zeros((1, TQ), F32)
            for m in range(nb):
                if m == n:
                    continue
                beats = (g[m] >= g[n]) if m < n else (g[m] > g[n])
                rank = rank + jnp.where(beats, jnp.where(m < own, 1.0, 0.0), 0.0)
            chosen = jnp.where(rank < MOBA_TOPK, 0.0, NEG_BIG)
            rows.append(jnp.where(n < own, chosen, jnp.where(n == own, 0.0, NEG_BIG)))
        rows += [jnp.full((1, TQ), NEG_BIG, F32)] * (8 - nb)
        selb_ref[h] = jnp.concatenate(rows, axis=0)

    def score_body(j, maxes):
        d, s0 = _tile_pos(j, jmax)
        n = j // (MOBA_BLOCK // TK)
        out = []
        for h in range(N_HEADS):
            cols = slice(h * HEAD_DIM, (h + 1) * HEAD_DIM)
            z = _dot_nt(k_ref[0, pl.ds(s0, TK), cols], q_ref[0, :, cols])
            z = z + tb_ref[h, d] + selb_ref[h, pl.ds(n, 1), :]
            z_s[h, pl.ds(s0, TK), :] = z
            out.append(jnp.maximum(maxes[h], _fold8(z, jnp.max)))
        return tuple(out)

    maxes = _tile_loop(jmax + 1, score_body, _neg_rows())
    _softmax_passes(z_s, p_s, maxes, vt_ref, o_ref, acc_ref, jmax + 1)


def _fox_kernel(q_ref, k_ref, vt_ref, tb_ref, o_ref, acc_ref, z_s, p_s):
    i = pl.program_id(1)
    jmax = (i + 1) * Q_PER_K - 1

    def score_body(j, maxes):
        d, s0 = _tile_pos(j, jmax)
        causal = tb_ref[0, d]
        out = []
        for h in range(N_HEADS):
            aug = slice(h * 128, (h + 1) * 128)
            z = _dot_nt(k_ref[0, pl.ds(s0, TK), aug], q_ref[0, :, aug]) + causal
            z_s[h, pl.ds(s0, TK), :] = z
            out.append(jnp.maximum(maxes[h], _fold8(z, jnp.max)))
        return tuple(out)

    maxes = _tile_loop(jmax + 1, score_body, _neg_rows())
    _softmax_passes(z_s, p_s, maxes, vt_ref, o_ref, acc_ref, jmax + 1)


def _dsa_kernel(q_ref, iq_ref, kv_ref, vt_ref, iw_ref, tb_ref, tc_ref, o_ref,
                acc_ref, z_s, p_s, keys_ref, *, topk):
    i = pl.program_id(1)
    jmax = (i + 1) * Q_PER_K - 1

    def index_body(j, carry):
        d, s0 = _tile_pos(j, jmax)
        k_idx = kv_ref[0, pl.ds(s0, TK), HEAD_DIM:HEAD_DIM + IDX_DIM]
        sc = jnp.zeros((TK, TQ), F32)
        for hh in range(IDX_HEADS):
            qi = iq_ref[0, :, hh * IDX_DIM:(hh + 1) * IDX_DIM]
            sc = sc + iw_ref[0, hh:hh + 1, :] * jnp.maximum(_dot_nt(k_idx, qi), 0.0)
        sc = jnp.where(sc == 0.0, 0.0, sc)
        bits = lax.bitcast_convert_type(sc, I32)
        key = bits ^ ((bits >> 31) & 0x7FFFFFFF)
        keys_ref[j] = jnp.where(tc_ref[0, d] < 0.0, INT_MIN, key)
        return carry

    _tile_loop(jmax + 1, index_body, 0)

    def bit_body(it, t_u):
        cand = t_u | (jnp.int32(1) << (31 - it))
        thr_c = cand ^ INT_MIN

        def count_body(j, cnt):
            return cnt + _fold8(jnp.where(keys_ref[j] >= thr_c, 1, 0), jnp.sum)

        cnt = lax.fori_loop(0, jmax + 1, count_body, jnp.zeros((8, TQ), I32))
        cnt = jnp.sum(cnt, axis=0, keepdims=True)
        return jnp.where(cnt >= topk, cand, t_u)

    thr = lax.fori_loop(0, 32, bit_body, jnp.zeros((1, TQ), I32)) ^ INT_MIN

    def score_body(j, maxes):
        d, s0 = _tile_pos(j, jmax)
        k_j = kv_ref[0, pl.ds(s0, TK), 0:HEAD_DIM]
        selb = jnp.where(keys_ref[j] >= thr, 0.0, NEG_BIG)
        out = []
        for h in range(N_HEADS):
            cols = slice(h * HEAD_DIM, (h + 1) * HEAD_DIM)
            z = _dot_nt(k_j, q_ref[0, :, cols]) + tb_ref[h, d] + selb
            z_s[h, pl.ds(s0, TK), :] = z
            out.append(jnp.maximum(maxes[h], _fold8(z, jnp.max)))
        return tuple(out)

    maxes = _tile_loop(jmax + 1, score_body, _neg_rows())
    _softmax_passes(z_s, p_s, maxes, vt_ref, o_ref, acc_ref, jmax + 1, shared_v=True)


def _mixer_call(kernel_fn, name, b, s, in_specs, scratch, args):
    return pl.pallas_call(
        kernel_fn,
        grid=(b, s // TQ),
        in_specs=in_specs,
        out_specs=pl.BlockSpec((1, TQ, BRANCH_W), lambda bi, i: (bi, i, 0)),
        out_shape=jax.ShapeDtypeStruct((b, s, BRANCH_W), BF16),
        scratch_shapes=scratch,
        compiler_params=_params("arbitrary", "arbitrary"),
        name=name,
    )(*args)


def _q_spec(width, col):
    return pl.BlockSpec((1, TQ, width), lambda bi, i: (bi, i, col))


def _k_spec(s, width, col):
    return pl.BlockSpec((1, s, width), lambda bi, i: (bi, 0, col))


def _vt_spec(s, rows, row_blk):
    return pl.BlockSpec((1, rows, s), lambda bi, i: (bi, row_blk, 0))


def _tile_spec(n, blk):
    return pl.BlockSpec((n, N_DIAG + 1, TK, TQ), lambda bi, i: (blk, 0, 0, 0))


_ACC = lambda: pltpu.VMEM((BRANCH_W, TQ), F32)
_SCORES = lambda s: pltpu.VMEM((N_HEADS, s, TQ), F32)
_WEIGHTS = lambda s: pltpu.VMEM((N_HEADS, s, TQ), BF16)


def _sb(main, vt, tiles):
    b, s, _ = main.shape
    return _mixer_call(
        _sb_kernel, "sb_mixer", b, s,
        [_q_spec(BRANCH_W, 0), _k_spec(s, BRANCH_W, 1), _vt_spec(s, BRANCH_W, 0),
         _tile_spec(2, KIND_STRICT_ADD // 2)],
        [_ACC(), _SCORES(s), pltpu.VMEM((N_HEADS, s // TK, 2 * TK, TQ), BF16), _WEIGHTS(s)],
        (main, main, vt, tiles))


def _moba(main, vt, tiles):
    b, s, _ = main.shape
    return _mixer_call(
        _moba_kernel, "moba_mixer", b, s,
        [_q_spec(BRANCH_W, 2), _k_spec(s, BRANCH_W, 3), _vt_spec(s, BRANCH_W, 1),
         _tile_spec(N_HEADS, 0)],
        [_ACC(), _SCORES(s), _WEIGHTS(s), pltpu.VMEM((N_HEADS, 8, TQ), F32)], (main, main, vt, tiles))


def _fox(q_aug, k_aug, vt, tiles):
    b, s, _ = q_aug.shape
    return _mixer_call(
        _fox_kernel, "fox_mixer", b, s,
        [_q_spec(N_HEADS * 128, 0), _k_spec(s, N_HEADS * 128, 0), _vt_spec(s, BRANCH_W, 2),
         _tile_spec(1, KIND_CAUSAL)],
        [_ACC(), _SCORES(s), _WEIGHTS(s)], (q_aug, k_aug, vt, tiles))


def _dsa(main, kv, vt, iw, tiles):
    b, s, _ = main.shape
    topk = min(DSA_TOPK, s // 4)
    return _mixer_call(
        functools.partial(_dsa_kernel, topk=topk), "dsa_mixer", b, s,
        [_q_spec(BRANCH_W, 6), _q_spec(BRANCH_W, 7), _k_spec(s, 128, 0),
         _vt_spec(s, HEAD_DIM, 3 * BRANCH_W // HEAD_DIM),
         pl.BlockSpec((1, 16, TQ), lambda bi, i: (bi, 0, i)),
         _tile_spec(N_HEADS, 1), _tile_spec(1, KIND_CAUSAL)],
        [_ACC(), _SCORES(s), _WEIGHTS(s), pltpu.VMEM((s // TK, TK, TQ), I32)],
        (main, main, kv, vt, iw, tiles, tiles))


def _fox_augment(main, pieces):
    b, s, _ = main.shape
    q = main[:, :, 4 * BRANCH_W:5 * BRANCH_W].reshape(b, s, N_HEADS, HEAD_DIM)
    k = main[:, :, 5 * BRANCH_W:6 * BRANCH_W].reshape(b, s, N_HEADS, HEAD_DIM)
    c = pieces[:, :, :3 * N_HEADS].reshape(b, s, 3, N_HEADS).transpose(0, 1, 3, 2)
    one = jnp.ones((b, s, N_HEADS, 3), BF16)
    pad = jnp.zeros((b, s, N_HEADS, 128 - HEAD_DIM - 6), BF16)
    q_aug = jnp.concatenate([q, -one, c, pad], axis=-1).reshape(b, s, N_HEADS * 128)
    k_aug = jnp.concatenate([k, c, one, pad], axis=-1).reshape(b, s, N_HEADS * 128)
    return q_aug, k_aug


def _combine_kernel(x_ref, g_ref, o0, o1, o2, o3, wg_ref, wb_ref, wo_ref, out_ref):
    x = x_ref[...]
    xn = _rmsnorm(x, g_ref[...]).astype(BF16)
    mixed = None
    for i, o_ref in enumerate((o0, o1, o2, o3)):
        gate = 1.0 / (1.0 + jnp.exp(-_dot(xn, wg_ref[i])))
        y = gate * _dot(o_ref[...], wb_ref[i])
        mixed = y if mixed is None else mixed + y
    out_ref[...] = x + _dot(mixed.astype(BF16), wo_ref[...])


def _combine(x2, g, outs, wg, wb, wo):
    t, d = x2.shape
    tm = 256
    row = lambda i: (i, 0)
    return pl.pallas_call(
        _combine_kernel,
        grid=(t // tm,),
        in_specs=[pl.BlockSpec((tm, d), row), pl.BlockSpec((1, d), lambda i: (0, 0))]
        + [pl.BlockSpec((tm, BRANCH_W), row)] * N_BRANCH
        + [pl.BlockSpec((N_BRANCH, d, d), lambda i: (0, 0, 0)),
           pl.BlockSpec((N_BRANCH, BRANCH_W, d), lambda i: (0, 0, 0)),
           pl.BlockSpec((d, d), lambda i: (0, 0))],
        out_specs=pl.BlockSpec((tm, d), row),
        out_shape=jax.ShapeDtypeStruct((t, d), F32),
        compiler_params=_params("arbitrary"),
        name="gate_combine_out",
    )(x2, g.reshape(1, d), *outs, wg, wb, wo)


FFN_CHUNK = 256
HALO = 16


def _gelu(g):
    return 0.5 * g * (1.0 + lax.erf(g * (2.0 ** -0.5)))


def _ffn_kernel(x_ref, xp_ref, g_ref, wg_ref, wu_ref, cw_ref, wd_ref, o_ref, xn_s, hg_s, hu_s, acc_s):
    i = pl.program_id(1)
    tm = x_ref.shape[1]
    x = x_ref[0]
    gain = g_ref[...]
    prev = _rmsnorm(xp_ref[0], gain) * jnp.where(i > 0, 1.0, 0.0)
    xn_s[...] = jnp.concatenate([jnp.zeros_like(prev), prev, _rmsnorm(x, gain)], axis=0).astype(BF16)
    acc_s[...] = jnp.zeros_like(acc_s)

    def taps(h_s, cw, r):
        out = cw[r + 3:r + 4, :]
        for j in range(CONV_W):
            out = out + cw[r + j:r + j + 1, :] * h_s[HALO - (CONV_W - 1) + j:HALO - (CONV_W - 1) + j + tm, :]
        return out

    for c in range(wg_ref.shape[0]):
        xn = xn_s[...]
        hg = hg_s.at[c % 2]
        hu = hu_s.at[c % 2]
        hg[...] = _dot(xn, wg_ref[c])
        hu[...] = _dot(xn, wu_ref[c])
        cw = cw_ref[c]
        a = _gelu(taps(hg, cw, 0)) * taps(hu, cw, 4)
        acc_s[...] += _dot(a.astype(BF16), wd_ref[c])
    o_ref[0] = x + acc_s[...]


def _ffn(x, g, wg, wu, cw, wd):
    b, s, d = x.shape
    tm = 256
    nc = wg.shape[0]
    const3 = lambda bi, i: (0, 0, 0)
    return pl.pallas_call(
        _ffn_kernel,
        grid=(b, s // tm),
        in_specs=[
            pl.BlockSpec((1, tm, d), lambda bi, i: (bi, i, 0)),
            pl.BlockSpec((1, 8, d), lambda bi, i: (bi, jnp.maximum(i * (tm // 8) - 1, 0), 0)),
            pl.BlockSpec((1, d), lambda bi, i: (0, 0)),
            pl.BlockSpec((nc, d, FFN_CHUNK), const3),
            pl.BlockSpec((nc, d, FFN_CHUNK), const3),
            pl.BlockSpec((nc, 8, FFN_CHUNK), const3),
            pl.BlockSpec((nc, FFN_CHUNK, d), const3),
        ],
        out_specs=pl.BlockSpec((1, tm, d), lambda bi, i: (bi, i, 0)),
        out_shape=jax.ShapeDtypeStruct((b, s, d), F32),
        scratch_shapes=[pltpu.VMEM((tm + HALO, d), BF16),
                        pltpu.VMEM((2, tm + HALO, FFN_CHUNK), F32),
                        pltpu.VMEM((2, tm + HALO, FFN_CHUNK), F32),
                        pltpu.VMEM((tm, d), F32)],
        compiler_params=_params("arbitrary", "arbitrary"),
        name="conv_ffn",
    )(x, x, g.reshape(1, d), wg, wu, cw, wd)


def _pack_ffn_weights(w_up, conv_w, conv_b, w_down):
    d, two_f = w_up.shape
    f = two_f // 2
    nc = f // FFN_CHUNK
    chunks = lambda w: w.reshape(w.shape[0], nc, FFN_CHUNK).transpose(1, 0, 2)
    wg = chunks(w_up[:, :f]).astype(BF16)
    wu = chunks(w_up[:, f:]).astype(BF16)
    cw = jnp.concatenate([conv_w[:, :f], conv_b[None, :f], conv_w[:, f:], conv_b[None, f:]], axis=0)
    wd = w_down.reshape(nc, FFN_CHUNK, d).astype(BF16)
    return wg, wu, chunks(cw), wd


def _norm_kernel(x_ref, g_ref, o_ref):
    o_ref[...] = _rmsnorm(x_ref[...], g_ref[...])


def _final_norm(x2, g):
    t, d = x2.shape
    tm = 512
    return pl.pallas_call(
        _norm_kernel,
        grid=(t // tm,),
        in_specs=[pl.BlockSpec((tm, d), lambda i: (i, 0)), pl.BlockSpec((1, d), lambda i: (0, 0))],
        out_specs=pl.BlockSpec((tm, d), lambda i: (i, 0)),
        out_shape=jax.ShapeDtypeStruct((t, d), F32),
        compiler_params=_params("arbitrary"),
        name="final_norm",
    )(x2, g.reshape(1, d))


def _token_mixers(x, g, w_in, fox_b, tiles):
    w_nat, w_t = _pack_in_weights(w_in)
    main, kv, fxf, vt, iw = _proj(x, g, w_nat, w_t)
    pieces = _fox_c(fxf, fox_b)
    q_aug, k_aug = _fox_augment(main, pieces)
    return (_sb(main, vt, tiles), _moba(main, vt, tiles), _fox(q_aug, k_aug, vt, tiles),
            _dsa(main, kv, vt, iw, tiles))


def kernel(x, norm_mix_g, norm_ffn_g, norm_final_g, w_in, fox_b_f, t5_bias, w_gate, w_branch, w_out,
           w_up, conv_w, conv_b, w_down):
    b, s, d = x.shape
    assert s % TQ == 0 and TQ == MOBA_BLOCK and s // MOBA_BLOCK <= 8
    tiles = _make_tiles(t5_bias)
    for l in range(w_in.shape[0]):
        outs = _token_mixers(x, norm_mix_g[l], w_in[l], fox_b_f[l], tiles)
        outs = [o.reshape(b * s, BRANCH_W) for o in outs]
        x = _combine(x.reshape(b * s, d), norm_mix_g[l], outs, w_gate[l].astype(BF16),
                     w_branch[l].astype(BF16), w_out[l].astype(BF16)).reshape(b, s, d)
        x = _ffn(x, norm_ffn_g[l], *_pack_ffn_weights(w_up[l], conv_w[l], conv_b[l], w_down[l]))
    return _final_norm(x.reshape(b * s, d), norm_final_g).reshape(b, s, d)
```

```python
import functools
import math

import jax
import jax.numpy as jnp
from jax import lax
from jax.experimental import pallas as pl
from jax.experimental.pallas import tpu as pltpu

F32 = jnp.float32
BF16 = jnp.bfloat16
I32 = jnp.int32

HEAD_DIM = 64
N_HEADS = 4
BRANCH_W = N_HEADS * HEAD_DIM
N_BRANCH = 4
MOBA_BLOCK = 256
MOBA_TOPK = 3
DSA_TOPK = 256
IDX_HEADS = 8
IDX_DIM = 32
T5_BUCKETS = 32
T5_MAX_DIST = 128
CONV_W = 3
RMS_EPS = 1e-6
NEG_BIG = -1e30
INT_MIN = -2147483648
LOG2E = math.log2(math.e)

TQ = 256
TK = 128
Q_PER_K = TQ // TK
T5_CONST_DIST = 113
N_DIAG = (TQ + T5_CONST_DIST - 2) // TK + 1
FAR_DIST = 1 << 20

KIND_CAUSAL = 8
KIND_STRICT_ADD = 10
KIND_STRICT_MUL = 11
N_KINDS = 12

VMEM_LIMIT = 56 * 1024 * 1024


def _dot(a, b):
    return jnp.dot(a, b, preferred_element_type=F32)


def _dot_nt(a, b):
    return lax.dot_general(a, b, (((1,), (1,)), ((), ())), preferred_element_type=F32)


def _rmsnorm(x, g):
    return x * lax.rsqrt(jnp.mean(x * x, axis=-1, keepdims=True) + RMS_EPS) * g


def _softplus(z):
    return jnp.maximum(z, 0.0) + jnp.log1p(jnp.exp(-jnp.abs(z)))


def _params(*sem):
    return pltpu.CompilerParams(dimension_semantics=sem, vmem_limit_bytes=VMEM_LIMIT)


def _tiles_kernel(tab_ref, out_ref):
    kind = pl.program_id(0)
    d = pl.program_id(1)
    sl = lax.broadcasted_iota(I32, (TK, TQ), 0)
    tl = lax.broadcasted_iota(I32, (TK, TQ), 1)
    dist = jnp.where(d == N_DIAG, FAR_DIST, (TK - TQ) + d * TK + tl - sl)
    n = jnp.maximum(dist, 0)
    max_exact = T5_BUCKETS // 2
    nf = jnp.maximum(n, 1).astype(F32)
    large = max_exact + (jnp.log(nf / max_exact) / math.log(T5_MAX_DIST / max_exact)
                         * (T5_BUCKETS - max_exact)).astype(I32)
    large = jnp.minimum(large, T5_BUCKETS - 1)
    bucket = jnp.where(n < max_exact, n, large)
    h = jnp.minimum(kind, 2 * N_HEADS - 1)
    bias = jnp.zeros((TK, TQ), F32)
    for b in range(T5_BUCKETS):
        bias = jnp.where(bucket == b, tab_ref[b, h], bias)
    causal = dist >= 0
    strict = dist > 0
    v_bias = jnp.where(causal, bias, NEG_BIG)
    v_causal = jnp.where(causal, 0.0, NEG_BIG)
    v_sadd = jnp.where(strict, 0.0, NEG_BIG)
    v_smul = jnp.where(strict, -1.0, 0.0)
    out_ref[0, 0] = jnp.where(kind < KIND_CAUSAL, v_bias,
                              jnp.where(kind < KIND_STRICT_ADD, v_causal,
                                        jnp.where(kind == KIND_STRICT_ADD, v_sadd, v_smul)))


def _make_tiles(t5_bias):
    return pl.pallas_call(
        _tiles_kernel,
        grid=(N_KINDS, N_DIAG + 1),
        in_specs=[pl.BlockSpec(memory_space=pltpu.SMEM)],
        out_specs=pl.BlockSpec((1, 1, TK, TQ), lambda k, d: (k, d, 0, 0)),
        out_shape=jax.ShapeDtypeStruct((N_KINDS, N_DIAG + 1, TK, TQ), F32),
        compiler_params=_params("arbitrary", "arbitrary"),
        name="t5_tiles",
    )(t5_bias)


N_MAIN = 8 * BRANCH_W
N_NAT = N_MAIN + 256
N_VT = 3 * BRANCH_W + HEAD_DIM
N_T = N_VT + 16
PROJ_CHUNK = 512


def _proj_kernel(x_ref, g_ref, w_ref, wt_ref, main_ref, kv_ref, fxf_ref, vt_ref, iw_ref):
    xn = _rmsnorm(x_ref[0], g_ref[...]).astype(BF16)
    for c in range(0, N_MAIN, PROJ_CHUNK):
        main_ref[0, :, c:c + PROJ_CHUNK] = _dot(xn, w_ref[:, c:c + PROJ_CHUNK]).astype(BF16)
    r = _dot(xn, w_ref[:, N_MAIN:N_NAT])
    kv_ref[0] = r[:, :128].astype(BF16)
    fxf_ref[0] = r[:, 128:]
    t = _dot_nt(wt_ref[...], xn)
    vt_ref[0] = t[:N_VT].astype(BF16)
    iw_ref[0] = t[N_VT:]


def _proj(x, g, w_nat, w_t):
    b, s, d = x.shape
    tm = min(512, s)
    return pl.pallas_call(
        _proj_kernel,
        grid=(b, s // tm),
        in_specs=[
            pl.BlockSpec((1, tm, d), lambda bi, i: (bi, i, 0)),
            pl.BlockSpec((1, d), lambda bi, i: (0, 0)),
            pl.BlockSpec((d, N_NAT), lambda bi, i: (0, 0)),
            pl.BlockSpec((N_T, d), lambda bi, i: (0, 0)),
        ],
        out_specs=[
            pl.BlockSpec((1, tm, N_MAIN), lambda bi, i: (bi, i, 0)),
            pl.BlockSpec((1, tm, 128), lambda bi, i: (bi, i, 0)),
            pl.BlockSpec((1, tm, 128), lambda bi, i: (bi, i, 0)),
            pl.BlockSpec((1, N_VT, tm), lambda bi, i: (bi, 0, i)),
            pl.BlockSpec((1, 16, tm), lambda bi, i: (bi, 0, i)),
        ],
        out_shape=[
            jax.ShapeDtypeStruct((b, s, N_MAIN), BF16),
            jax.ShapeDtypeStruct((b, s, 128), BF16),
            jax.ShapeDtypeStruct((b, s, 128), F32),
            jax.ShapeDtypeStruct((b, N_VT, s), BF16),
            jax.ShapeDtypeStruct((b, 16, s), F32),
        ],
        compiler_params=_params("arbitrary", "arbitrary"),
        name="norm_in_proj",
    )(x, g.reshape(1, d), w_nat, w_t)


def _pack_in_weights(w_in):
    d = w_in.shape[0]
    sizes = ([BRANCH_W] * 9 + [N_HEADS] + [BRANCH_W, HEAD_DIM, HEAD_DIM]
             + [IDX_HEADS * IDX_DIM, IDX_DIM, IDX_HEADS])
    offs = [0]
    for sz in sizes:
        offs.append(offs[-1] + sz)
    seg = [w_in[:, offs[i]:offs[i + 1]] for i in range(len(sizes))]
    (sb_q, sb_k, sb_v, mb_q, mb_k, mb_v, fx_q, fx_k, fx_v, fx_f,
     ds_q, ds_k, ds_v, ix_q, ix_k, ix_w) = seg
    scale = HEAD_DIM ** -0.5
    z = lambda n: jnp.zeros((d, n), w_in.dtype)
    w_nat = jnp.concatenate(
        [sb_q * scale, sb_k, mb_q * scale, mb_k, fx_q * scale, fx_k, ds_q * scale, ix_q,
         ds_k, ix_k, z(128 - HEAD_DIM - IDX_DIM),
         fx_f, fx_f, fx_f, z(128 - 3 * N_HEADS)], axis=1).astype(BF16)
    w_t = jnp.concatenate([sb_v, mb_v, fx_v, ds_v, ix_w, z(16 - IDX_HEADS)], axis=1).T.astype(BF16)
    return w_nat, w_t


CUM_BLOCK = 256


def _fox_c_kernel(f_ref, b_ref, p_ref):
    s = f_ref.shape[1]
    row = lax.broadcasted_iota(I32, (CUM_BLOCK, CUM_BLOCK), 0)
    col = lax.broadcasted_iota(I32, (CUM_BLOCK, CUM_BLOCK), 1)
    tri = jnp.where(row >= col, 1.0, 0.0).astype(BF16)
    tri3 = jnp.concatenate([tri, tri, tri], axis=1)
    lane = lax.broadcasted_iota(I32, (CUM_BLOCK, 128), 1)
    carry = jnp.zeros((1, 128), F32)
    for blk in range(s // CUM_BLOCK):
        f = f_ref[0, blk * CUM_BLOCK:(blk + 1) * CUM_BLOCK, :] + b_ref[...]
        ls = -_softplus(-f)
        hi = ls.astype(BF16)
        r1 = ls - hi.astype(F32)
        mid = r1.astype(BF16)
        lo = (r1 - mid.astype(F32)).astype(BF16)
        c = _dot(tri3, jnp.concatenate([hi, mid, lo], axis=0)) + carry
        carry = c[CUM_BLOCK - 1:CUM_BLOCK, :]
        c_hi = c.astype(BF16).astype(F32)
        r2 = c - c_hi
        c_mid = r2.astype(BF16).astype(F32)
        c_lo = r2 - c_mid
        piece = jnp.where(lane < N_HEADS, c_hi, jnp.where(lane < 2 * N_HEADS, c_mid, c_lo))
        p_ref[0, blk * CUM_BLOCK:(blk + 1) * CUM_BLOCK, :] = piece.astype(BF16)


def _fox_c(fxf, fox_b):
    b, s, _ = fxf.shape
    bias = jnp.concatenate([fox_b, fox_b, fox_b, jnp.zeros((128 - 3 * N_HEADS,), F32)]).reshape(1, 128)
    return pl.pallas_call(
        _fox_c_kernel,
        grid=(b,),
        in_specs=[pl.BlockSpec((1, s, 128), lambda bi: (bi, 0, 0)),
                  pl.BlockSpec((1, 128), lambda bi: (0, 0))],
        out_specs=pl.BlockSpec((1, s, 128), lambda bi: (bi, 0, 0)),
        out_shape=jax.ShapeDtypeStruct((b, s, 128), BF16),
        compiler_params=_params("arbitrary"),
        name="fox_cumgate",
    )(fxf, bias)


def _tile_pos(j, jmax):
    return jnp.minimum(jmax - j, N_DIAG), pl.multiple_of(j * TK, TK)


def _tile_loop(n_tiles, body, init):
    def trip(t, carry):
        for u in range(Q_PER_K):
            carry = body(t * Q_PER_K + u, carry)
        return carry

    return lax.fori_loop(0, n_tiles // Q_PER_K, trip, init)


def _fold8(x, op):
    return op(x.reshape(TK // 8, 8, TQ), axis=0)


def _neg_rows():
    return tuple(jnp.full((8, TQ), -1e38, F32) for _ in range(N_HEADS))


def _weights_matmul(w_s, vt_ref, o_ref, acc_ref, n_tiles, inv_l, shared_v):
    def zero_body(j, carry):
        s0 = pl.multiple_of(j * TK, TK)
        for h in range(N_HEADS):
            w_s[h, pl.ds(s0, TK), :] = jnp.zeros((TK, TQ), BF16)
        return carry

    lax.fori_loop(n_tiles, w_s.shape[1] // TK, zero_body, 0)
    for h in range(N_HEADS):
        rows = slice(h * HEAD_DIM, (h + 1) * HEAD_DIM)
        out = _dot(vt_ref[0] if shared_v else vt_ref[0, rows, :], w_s[h])
        acc_ref[rows, :] = out if inv_l is None else out * inv_l[h]
    o_ref[0] = acc_ref[...].T.astype(o_ref.dtype)


def _softmax_passes(z_s, p_s, maxes, vt_ref, o_ref, acc_ref, n_tiles, shared_v=False):
    m = [jnp.max(mx, axis=0, keepdims=True) for mx in maxes]

    def exp_body(j, sums):
        s0 = pl.multiple_of(j * TK, TK)
        out = []
        for h in range(N_HEADS):
            p = jnp.exp2(z_s[h, pl.ds(s0, TK), :] - m[h])
            p_s[h, pl.ds(s0, TK), :] = p.astype(BF16)
            out.append(sums[h] + _fold8(p, jnp.sum))
        return tuple(out)

    sums = lax.fori_loop(0, n_tiles, exp_body, tuple(jnp.zeros((8, TQ), F32) for _ in range(N_HEADS)))
    inv_l = [1.0 / jnp.sum(sm, axis=0, keepdims=True) for sm in sums]
    _weights_matmul(p_s, vt_ref, o_ref, acc_ref, n_tiles, inv_l, shared_v)


def _sb_kernel(q_ref, k_ref, vt_ref, tb_ref, o_ref, acc_ref, base_s, hl_s, w_s):
    i = pl.program_id(1)
    jmax = (i + 1) * Q_PER_K - 1

    def score_body(j, carry):
        d, s0 = _tile_pos(j, jmax)
        sadd = tb_ref[0, d]
        smul = tb_ref[1, d]
        for h in range(N_HEADS):
            cols = slice(h * HEAD_DIM, (h + 1) * HEAD_DIM)
            z = _dot_nt(k_ref[0, pl.ds(s0, TK), cols], q_ref[0, :, cols])
            sp = jnp.maximum(z, 0.0) + jnp.log(1.0 + jnp.exp(-jnp.abs(z)))
            base_s[h, pl.ds(s0, TK), :] = z - sp + sadd
            lneg = sp * smul
            hi = lneg.astype(BF16)
            lo = (lneg - hi.astype(F32)).astype(BF16)
            hl_s[h, j] = jnp.concatenate([hi, lo], axis=0)
        return carry

    _tile_loop(jmax + 1, score_body, 0)

    row = lax.broadcasted_iota(I32, (TK + 16, 2 * TK), 0)
    col = lax.broadcasted_iota(I32, (TK + 16, 2 * TK), 1)
    upper = jnp.where(((col & (TK - 1)) > row) | (row >= TK), 1.0, 0.0).astype(BF16)

    def weight_body(jj, suffix):
        j = jmax - jj
        s0 = pl.multiple_of(j * TK, TK)
        out = []
        for h in range(N_HEADS):
            r = _dot(upper, hl_s[h, j])
            w = jnp.exp(base_s[h, pl.ds(s0, TK), :] + r[:TK] + suffix[h])
            w_s[h, pl.ds(s0, TK), :] = w.astype(BF16)
            out.append(suffix[h] + r[TK:TK + 1])
        return tuple(out)

    _tile_loop(jmax + 1, weight_body, tuple(jnp.zeros((1, TQ), F32) for _ in range(N_HEADS)))
    _weights_matmul(w_s, vt_ref, o_ref, acc_ref, jmax + 1, None, False)


def _moba_kernel(q_ref, k_ref, vt_ref, tb_ref, o_ref, acc_ref, z_s, p_s, selb_ref, km_s):
    i = pl.program_id(1)
    own = i
    jmax = (i + 1) * Q_PER_K - 1
    nb = k_ref.shape[1] // MOBA_BLOCK

    @pl.when(i == 0)
    def _():
        s_len = k_ref.shape[1]
        blk = lax.broadcasted_iota(I32, (8, s_len), 1) >> (MOBA_BLOCK.bit_length() - 1)
        avg = jnp.where(blk == lax.broadcasted_iota(I32, (8, s_len), 0), 1.0 / MOBA_BLOCK, 0.0)
        km_s[...] = _dot(avg.astype(BF16), k_ref[0])

    km_all = km_s[...]
    n_idx = lax.broadcasted_iota(I32, (8, TQ), 0)
    for h in range(N_HEADS):
        cols = slice(h * HEAD_DIM, (h + 1) * HEAD_DIM)
        q_h = q_ref[0, :, cols]
        km = km_all[:, cols]
        km_hi = km.astype(BF16)
        km_lo = (km - km_hi.astype(F32)).astype(BF16)
        gate = _dot_nt(km_hi, q_h) + _dot_nt(km_lo, q_h)
        rank = jnp.zeros((8, TQ), F32)
        for m in range(nb):
            g_m = gate[m:m + 1, :]
            wins = jnp.where(n_idx > m, jnp.where(g_m >= gate, 1.0, 0.0), jnp.where(g_m > gate, 1.0, 0.0))
            rank = rank + wins * jnp.where(m < own, 1.0, 0.0)
        chosen = jnp.where(rank < MOBA_TOPK, 0.0, NEG_BIG)
        selb_ref[h] = jnp.where(n_idx < own, chosen, jnp.where(n_idx == own, 0.0, NEG_BIG))

    def score_body(j, maxes):
        d, s0 = _tile_pos(j, jmax)
        n = j // (MOBA_BLOCK // TK)
        out = []
        for h in range(N_HEADS):
            cols = slice(h * HEAD_DIM, (h + 1) * HEAD_DIM)
            z = _dot_nt(k_ref[0, pl.ds(s0, TK), cols], q_ref[0, :, cols])
            z = (z + tb_ref[h, d] + selb_ref[h, pl.ds(n, 1), :]) * LOG2E
            z_s[h, pl.ds(s0, TK), :] = z
            out.append(jnp.maximum(maxes[h], _fold8(z, jnp.max)))
        return tuple(out)

    maxes = _tile_loop(jmax + 1, score_body, _neg_rows())
    _softmax_passes(z_s, p_s, maxes, vt_ref, o_ref, acc_ref, jmax + 1)


def _fox_kernel(q_ref, k_ref, vt_ref, tb_ref, o_ref, acc_ref, z_s, p_s):
    i = pl.program_id(1)
    jmax = (i + 1) * Q_PER_K - 1

    def score_body(j, maxes):
        d, s0 = _tile_pos(j, jmax)
        causal = tb_ref[0, d]
        out = []
        for h in range(N_HEADS):
            aug = slice(h * 128, (h + 1) * 128)
            z = (_dot_nt(k_ref[0, pl.ds(s0, TK), aug], q_ref[0, :, aug]) + causal) * LOG2E
            z_s[h, pl.ds(s0, TK), :] = z
            out.append(jnp.maximum(maxes[h], _fold8(z, jnp.max)))
        return tuple(out)

    maxes = _tile_loop(jmax + 1, score_body, _neg_rows())
    _softmax_passes(z_s, p_s, maxes, vt_ref, o_ref, acc_ref, jmax + 1)


I16 = jnp.int16
HALF = 1 << 15


def _count16(ref, n_tiles, thr16, strict):
    def body(j, cnt):
        t = ref[j]
        hit = (t > thr16) if strict else (t >= thr16)
        g = jnp.where(hit, jnp.int16(1), jnp.int16(0)).reshape(TK // 16, 16, TQ)
        for r in range(TK // 16):
            cnt = cnt + g[r]
        return cnt

    cnt = _tile_loop(n_tiles, body, jnp.zeros((16, TQ), I16))
    return jnp.sum(cnt.astype(I32), axis=0, keepdims=True)


def _radix16(ref, n_tiles, need):
    def bit_body(it, t_u):
        cand = t_u | (jnp.int32(1) << (15 - it))
        cnt = _count16(ref, n_tiles, (cand - HALF).astype(I16), False)
        return jnp.where(cnt >= need, cand, t_u)

    return lax.fori_loop(0, 16, bit_body, jnp.zeros((1, TQ), I32))


def _dsa_kernel(q_ref, iq_ref, kv_ref, vt_ref, iw_ref, tb_ref, tc_ref, o_ref,
                acc_ref, z_s, p_s, keys_ref, khi_ref, klo_ref, *, topk):
    i = pl.program_id(1)
    jmax = (i + 1) * Q_PER_K - 1

    def index_body(j, carry):
        d, s0 = _tile_pos(j, jmax)
        k_idx = kv_ref[0, pl.ds(s0, TK), HEAD_DIM:HEAD_DIM + IDX_DIM]
        sc = jnp.zeros((TK, TQ), F32)
        for hh in range(IDX_HEADS):
            qi = iq_ref[0, :, hh * IDX_DIM:(hh + 1) * IDX_DIM]
            sc = sc + iw_ref[0, hh:hh + 1, :] * jnp.maximum(_dot_nt(k_idx, qi), 0.0)
        sc = jnp.where(sc == 0.0, 0.0, sc)
        bits = lax.bitcast_convert_type(sc, I32)
        key = bits ^ ((bits >> 31) & 0x7FFFFFFF)
        key = jnp.where(tc_ref[0, d] < 0.0, INT_MIN, key)
        keys_ref[j] = key
        khi_ref[j] = (key >> 16).astype(I16)
        klo_ref[j] = ((key & 0xFFFF) - HALF).astype(I16)
        return carry

    _tile_loop(jmax + 1, index_body, 0)

    n_tiles = jmax + 1
    t_hi = _radix16(khi_ref, n_tiles, topk)
    t_hi16 = (t_hi - HALF).astype(I16)
    need_lo = topk - _count16(khi_ref, n_tiles, t_hi16, True)

    def restrict_body(j, carry):
        klo_ref[j] = jnp.where(khi_ref[j] == t_hi16, klo_ref[j], jnp.int16(-HALF))
        return carry

    _tile_loop(n_tiles, restrict_body, 0)
    t_lo = _radix16(klo_ref, n_tiles, need_lo)
    thr = (t_hi - HALF) * (1 << 16) + t_lo
    n_above = (topk - need_lo) + _count16(klo_ref, n_tiles, (t_lo - HALF).astype(I16), True)
    need_eq = (topk - n_above).astype(F32)

    row = lax.broadcasted_iota(I32, (TK + 16, TK), 0)
    col = lax.broadcasted_iota(I32, (TK + 16, TK), 1)
    lower = jnp.where((col < row) | (row >= TK), 1.0, 0.0).astype(BF16)

    def score_body(j, carry):
        maxes, seen = carry
        d, s0 = _tile_pos(j, jmax)
        k_j = kv_ref[0, pl.ds(s0, TK), 0:HEAD_DIM]
        key = keys_ref[j]
        is_thr = key == thr
        r = _dot(lower, jnp.where(is_thr, 1.0, 0.0).astype(BF16))
        tie = jnp.where(r[:TK] + seen < need_eq, 0.0, NEG_BIG)
        selb = jnp.where(key > thr, 0.0, jnp.where(is_thr, tie, NEG_BIG))
        out = []
        for h in range(N_HEADS):
            cols = slice(h * HEAD_DIM, (h + 1) * HEAD_DIM)
            z = (_dot_nt(k_j, q_ref[0, :, cols]) + tb_ref[h, d] + selb) * LOG2E
            z_s[h, pl.ds(s0, TK), :] = z
            out.append(jnp.maximum(maxes[h], _fold8(z, jnp.max)))
        return tuple(out), seen + r[TK:TK + 1]

    maxes, _ = _tile_loop(jmax + 1, score_body, (_neg_rows(), jnp.zeros((1, TQ), F32)))
    _softmax_passes(z_s, p_s, maxes, vt_ref, o_ref, acc_ref, jmax + 1, shared_v=True)


def _mixer_call(kernel_fn, name, b, s, in_specs, scratch, args):
    return pl.pallas_call(
        kernel_fn,
        grid=(b, s // TQ),
        in_specs=in_specs,
        out_specs=pl.BlockSpec((1, TQ, BRANCH_W), lambda bi, i: (bi, i, 0)),
        out_shape=jax.ShapeDtypeStruct((b, s, BRANCH_W), BF16),
        scratch_shapes=scratch,
        compiler_params=_params("arbitrary", "arbitrary"),
        name=name,
    )(*args)


def _q_spec(width, col):
    return pl.BlockSpec((1, TQ, width), lambda bi, i: (bi, i, col))


def _k_spec(s, width, col):
    return pl.BlockSpec((1, s, width), lambda bi, i: (bi, 0, col))


def _vt_spec(s, rows, row_blk):
    return pl.BlockSpec((1, rows, s), lambda bi, i: (bi, row_blk, 0))


def _tile_spec(n, blk):
    return pl.BlockSpec((n, N_DIAG + 1, TK, TQ), lambda bi, i: (blk, 0, 0, 0))


_ACC = lambda: pltpu.VMEM((BRANCH_W, TQ), F32)
_SCORES = lambda s: pltpu.VMEM((N_HEADS, s, TQ), F32)
_WEIGHTS = lambda s: pltpu.VMEM((N_HEADS, s, TQ), BF16)


def _sb(main, vt, tiles):
    b, s, _ = main.shape
    return _mixer_call(
        _sb_kernel, "sb_mixer", b, s,
        [_q_spec(BRANCH_W, 0), _k_spec(s, BRANCH_W, 1), _vt_spec(s, BRANCH_W, 0),
         _tile_spec(2, KIND_STRICT_ADD // 2)],
        [_ACC(), _SCORES(s), pltpu.VMEM((N_HEADS, s // TK, 2 * TK, TQ), BF16), _WEIGHTS(s)],
        (main, main, vt, tiles))


def _moba(main, vt, tiles):
    b, s, _ = main.shape
    return _mixer_call(
        _moba_kernel, "moba_mixer", b, s,
        [_q_spec(BRANCH_W, 2), _k_spec(s, BRANCH_W, 3), _vt_spec(s, BRANCH_W, 1),
         _tile_spec(N_HEADS, 0)],
        [_ACC(), _SCORES(s), _WEIGHTS(s), pltpu.VMEM((N_HEADS, 8, TQ), F32),
         pltpu.VMEM((8, BRANCH_W), F32)], (main, main, vt, tiles))


def _fox(q_aug, k_aug, vt, tiles):
    b, s, _ = q_aug.shape
    return _mixer_call(
        _fox_kernel, "fox_mixer", b, s,
        [_q_spec(N_HEADS * 128, 0), _k_spec(s, N_HEADS * 128, 0), _vt_spec(s, BRANCH_W, 2),
         _tile_spec(1, KIND_CAUSAL)],
        [_ACC(), _SCORES(s), _WEIGHTS(s)], (q_aug, k_aug, vt, tiles))


def _dsa(main, kv, vt, iw, tiles):
    b, s, _ = main.shape
    topk = min(DSA_TOPK, s // 4)
    return _mixer_call(
        functools.partial(_dsa_kernel, topk=topk), "dsa_mixer", b, s,
        [_q_spec(BRANCH_W, 6), _q_spec(BRANCH_W, 7), _k_spec(s, 128, 0),
         _vt_spec(s, HEAD_DIM, 3 * BRANCH_W // HEAD_DIM),
         pl.BlockSpec((1, 16, TQ), lambda bi, i: (bi, 0, i)),
         _tile_spec(N_HEADS, 1), _tile_spec(1, KIND_CAUSAL)],
        [_ACC(), _SCORES(s), _WEIGHTS(s), pltpu.VMEM((s // TK, TK, TQ), I32),
         pltpu.VMEM((s // TK, TK, TQ), I16), pltpu.VMEM((s // TK, TK, TQ), I16)],
        (main, main, kv, vt, iw, tiles, tiles))


def _fox_augment(main, pieces):
    b, s, _ = main.shape
    one = jnp.ones((b, s, 3), BF16)
    pad = jnp.zeros((b, s, 128 - HEAD_DIM - 6), BF16)
    q_parts, k_parts = [], []
    for h in range(N_HEADS):
        q = main[:, :, 4 * BRANCH_W + h * HEAD_DIM:4 * BRANCH_W + (h + 1) * HEAD_DIM]
        k = main[:, :, 5 * BRANCH_W + h * HEAD_DIM:5 * BRANCH_W + (h + 1) * HEAD_DIM]
        c = pieces[:, :, h:3 * N_HEADS:N_HEADS]
        q_parts += [q, -one, c, pad]
        k_parts += [k, c, one, pad]
    return jnp.concatenate(q_parts, axis=-1), jnp.concatenate(k_parts, axis=-1)


def _combine_kernel(x_ref, g_ref, o0, o1, o2, o3, wg_ref, wb_ref, wo_ref, out_ref):
    x = x_ref[...]
    xn = _rmsnorm(x, g_ref[...]).astype(BF16)
    mixed = None
    for i, o_ref in enumerate((o0, o1, o2, o3)):
        gate = 1.0 / (1.0 + jnp.exp(-_dot(xn, wg_ref[i])))
        y = gate * _dot(o_ref[...], wb_ref[i])
        mixed = y if mixed is None else mixed + y
    out_ref[...] = x + _dot(mixed.astype(BF16), wo_ref[...])


def _combine(x2, g, outs, wg, wb, wo):
    t, d = x2.shape
    tm = 256
    row = lambda i: (i, 0)
    return pl.pallas_call(
        _combine_kernel,
        grid=(t // tm,),
        in_specs=[pl.BlockSpec((tm, d), row), pl.BlockSpec((1, d), lambda i: (0, 0))]
        + [pl.BlockSpec((tm, BRANCH_W), row)] * N_BRANCH
        + [pl.BlockSpec((N_BRANCH, d, d), lambda i: (0, 0, 0)),
           pl.BlockSpec((N_BRANCH, BRANCH_W, d), lambda i: (0, 0, 0)),
           pl.BlockSpec((d, d), lambda i: (0, 0))],
        out_specs=pl.BlockSpec((tm, d), row),
        out_shape=jax.ShapeDtypeStruct((t, d), F32),
        compiler_params=_params("arbitrary"),
        name="gate_combine_out",
    )(x2, g.reshape(1, d), *outs, wg, wb, wo)


FFN_CHUNK = 256
HALO = 16


def _gelu(g):
    return 0.5 * g * (1.0 + lax.erf(g * (2.0 ** -0.5)))


def _ffn_kernel(x_ref, xp_ref, g_ref, gf_ref, wg_ref, wu_ref, cw_ref, wd_ref, o_ref, xn_s, hg_s, hu_s, *,
                final_norm):
    i = pl.program_id(1)
    tm = x_ref.shape[1]
    x = x_ref[0]
    gain = g_ref[...]
    prev = _rmsnorm(xp_ref[0], gain) * jnp.where(i > 0, 1.0, 0.0)
    xn_s[...] = jnp.concatenate([jnp.zeros_like(prev), prev, _rmsnorm(x, gain)], axis=0).astype(BF16)

    def taps(h_s, cw, r):
        out = cw[r + 3:r + 4, :]
        for j in range(CONV_W):
            out = out + cw[r + j:r + j + 1, :] * h_s[HALO - (CONV_W - 1) + j:HALO - (CONV_W - 1) + j + tm, :]
        return out

    nc = wg_ref.shape[0]

    def up(c):
        xn = xn_s[...]
        hg_s[c % 2] = _dot(xn, wg_ref[c])
        hu_s[c % 2] = _dot(xn, wu_ref[c])

    up(0)
    acc = x
    for c in range(nc):
        if c + 1 < nc:
            up(c + 1)
        cw = cw_ref[c]
        a = _gelu(taps(hg_s.at[c % 2], cw, 0)) * taps(hu_s.at[c % 2], cw, 4)
        acc = acc + _dot(a.astype(BF16), wd_ref[c])
    o_ref[0] = _rmsnorm(acc, gf_ref[...]) if final_norm else acc


def _ffn(x, g, g_final, final_norm, wg, wu, cw, wd):
    b, s, d = x.shape
    tm = 256
    nc = wg.shape[0]
    const3 = lambda bi, i: (0, 0, 0)
    return pl.pallas_call(
        functools.partial(_ffn_kernel, final_norm=final_norm),
        grid=(b, s // tm),
        in_specs=[
            pl.BlockSpec((1, tm, d), lambda bi, i: (bi, i, 0)),
            pl.BlockSpec((1, 8, d), lambda bi, i: (bi, jnp.maximum(i * (tm // 8) - 1, 0), 0)),
            pl.BlockSpec((1, d), lambda bi, i: (0, 0)),
            pl.BlockSpec((1, d), lambda bi, i: (0, 0)),
            pl.BlockSpec((nc, d, FFN_CHUNK), const3),
            pl.BlockSpec((nc, d, FFN_CHUNK), const3),
            pl.BlockSpec((nc, 8, FFN_CHUNK), const3),
            pl.BlockSpec((nc, FFN_CHUNK, d), const3),
        ],
        out_specs=pl.BlockSpec((1, tm, d), lambda bi, i: (bi, i, 0)),
        out_shape=jax.ShapeDtypeStruct((b, s, d), F32),
        scratch_shapes=[pltpu.VMEM((tm + HALO, d), BF16),
                        pltpu.VMEM((2, tm + HALO, FFN_CHUNK), F32),
                        pltpu.VMEM((2, tm + HALO, FFN_CHUNK), F32)],
        compiler_params=_params("arbitrary", "arbitrary"),
        name="conv_ffn",
    )(x, x, g.reshape(1, d), g_final.reshape(1, d), wg, wu, cw, wd)


def _pack_ffn_weights(w_up, conv_w, conv_b, w_down):
    d, two_f = w_up.shape
    f = two_f // 2
    nc = f // FFN_CHUNK
    chunks = lambda w: w.reshape(w.shape[0], nc, FFN_CHUNK).transpose(1, 0, 2)
    wg = chunks(w_up[:, :f]).astype(BF16)
    wu = chunks(w_up[:, f:]).astype(BF16)
    cw = jnp.concatenate([conv_w[:, :f], conv_b[None, :f], conv_w[:, f:], conv_b[None, f:]], axis=0)
    wd = w_down.reshape(nc, FFN_CHUNK, d).astype(BF16)
    return wg, wu, chunks(cw), wd


def _token_mixers(x, g, w_in, fox_b, tiles):
    w_nat, w_t = _pack_in_weights(w_in)
    main, kv, fxf, vt, iw = _proj(x, g, w_nat, w_t)
    pieces = _fox_c(fxf, fox_b)
    q_aug, k_aug = _fox_augment(main, pieces)
    return (_sb(main, vt, tiles), _moba(main, vt, tiles), _fox(q_aug, k_aug, vt, tiles),
            _dsa(main, kv, vt, iw, tiles))


def kernel(x, norm_mix_g, norm_ffn_g, norm_final_g, w_in, fox_b_f, t5_bias, w_gate, w_branch, w_out,
           w_up, conv_w, conv_b, w_down):
    b, s, d = x.shape
    assert s % TQ == 0 and TQ == MOBA_BLOCK and s // MOBA_BLOCK <= 8
    tiles = _make_tiles(t5_bias)
    depth = w_in.shape[0]
    for l in range(depth):
        outs = _token_mixers(x, norm_mix_g[l], w_in[l], fox_b_f[l], tiles)
        outs = [o.reshape(b * s, BRANCH_W) for o in outs]
        x = _combine(x.reshape(b * s, d), norm_mix_g[l], outs, w_gate[l].astype(BF16),
                     w_branch[l].astype(BF16), w_out[l].astype(BF16)).reshape(b, s, d)
        x = _ffn(x, norm_ffn_g[l], norm_final_g, l == depth - 1,
                 *_pack_ffn_weights(w_up[l], conv_w[l], conv_b[l], w_down[l]))
    return x
```

```python
import functools
import math

import jax
import jax.numpy as jnp
from jax import lax
from jax.experimental import pallas as pl
from jax.experimental.pallas import tpu as pltpu

F32 = jnp.float32
BF16 = jnp.bfloat16
I32 = jnp.int32

HEAD_DIM = 64
N_HEADS = 4
BRANCH_W = N_HEADS * HEAD_DIM
N_BRANCH = 4
MOBA_BLOCK = 256
MOBA_TOPK = 3
DSA_TOPK = 256
IDX_HEADS = 8
IDX_DIM = 32
T5_BUCKETS = 32
T5_MAX_DIST = 128
CONV_W = 3
RMS_EPS = 1e-6
NEG_BIG = -1e30
INT_MIN = -2147483648
LOG2E = math.log2(math.e)

TQ = 256
TK = 128
Q_PER_K = TQ // TK
T5_CONST_DIST = 113
N_DIAG = (TQ + T5_CONST_DIST - 2) // TK + 1
FAR_DIST = 1 << 20

KIND_CAUSAL = 8
KIND_STRICT_ADD = 10
KIND_STRICT_MUL = 11
N_KINDS = 12

VMEM_LIMIT = 56 * 1024 * 1024


def _dot(a, b):
    return jnp.dot(a, b, preferred_element_type=F32)


def _dot_nt(a, b):
    return lax.dot_general(a, b, (((1,), (1,)), ((), ())), preferred_element_type=F32)


def _rmsnorm(x, g):
    return x * lax.rsqrt(jnp.mean(x * x, axis=-1, keepdims=True) + RMS_EPS) * g


def _softplus(z):
    return jnp.maximum(z, 0.0) + jnp.log1p(jnp.exp(-jnp.abs(z)))


def _params(*sem):
    return pltpu.CompilerParams(dimension_semantics=sem, vmem_limit_bytes=VMEM_LIMIT)


def _tiles_kernel(tab_ref, out_ref):
    kind = pl.program_id(0)
    d = pl.program_id(1)
    sl = lax.broadcasted_iota(I32, (TK, TQ), 0)
    tl = lax.broadcasted_iota(I32, (TK, TQ), 1)
    dist = jnp.where(d == N_DIAG, FAR_DIST, (TK - TQ) + d * TK + tl - sl)
    n = jnp.maximum(dist, 0)
    max_exact = T5_BUCKETS // 2
    nf = jnp.maximum(n, 1).astype(F32)
    large = max_exact + (jnp.log(nf / max_exact) / math.log(T5_MAX_DIST / max_exact)
                         * (T5_BUCKETS - max_exact)).astype(I32)
    large = jnp.minimum(large, T5_BUCKETS - 1)
    bucket = jnp.where(n < max_exact, n, large)
    h = jnp.minimum(kind, 2 * N_HEADS - 1)
    bias = jnp.zeros((TK, TQ), F32)
    for b in range(T5_BUCKETS):
        bias = jnp.where(bucket == b, tab_ref[b, h], bias)
    causal = dist >= 0
    strict = dist > 0
    v_bias = jnp.where(causal, bias, NEG_BIG)
    v_causal = jnp.where(causal, 0.0, NEG_BIG)
    v_sadd = jnp.where(strict, 0.0, NEG_BIG)
    v_smul = jnp.where(strict, -1.0, 0.0)
    out_ref[0, 0] = jnp.where(kind < KIND_CAUSAL, v_bias,
                              jnp.where(kind < KIND_STRICT_ADD, v_causal,
                                        jnp.where(kind == KIND_STRICT_ADD, v_sadd, v_smul)))


def _make_tiles(t5_bias):
    return pl.pallas_call(
        _tiles_kernel,
        grid=(N_KINDS, N_DIAG + 1),
        in_specs=[pl.BlockSpec(memory_space=pltpu.SMEM)],
        out_specs=pl.BlockSpec((1, 1, TK, TQ), lambda k, d: (k, d, 0, 0)),
        out_shape=jax.ShapeDtypeStruct((N_KINDS, N_DIAG + 1, TK, TQ), F32),
        compiler_params=_params("arbitrary", "arbitrary"),
        name="t5_tiles",
    )(t5_bias)


N_MAIN = 8 * BRANCH_W
FOX_EXTRA = 6
N_NAT = N_MAIN + 128 + BRANCH_W
N_VT = 3 * BRANCH_W + HEAD_DIM
N_T = N_VT + 16
PROJ_CHUNK = 512


def _proj_kernel(x_ref, g_ref, w_ref, wt_ref, main_ref, kv_ref, fxf_ref, vt_ref, iw_ref):
    xn = _rmsnorm(x_ref[0], g_ref[...]).astype(BF16)
    for c in range(0, N_MAIN, PROJ_CHUNK):
        main_ref[0, :, c:c + PROJ_CHUNK] = _dot(xn, w_ref[:, c:c + PROJ_CHUNK]).astype(BF16)
    r = _dot(xn, w_ref[:, N_MAIN:N_NAT])
    kv_ref[0] = r[:, :128].astype(BF16)
    fxf_ref[0] = r[:, 128:]
    t = _dot_nt(wt_ref[...], xn)
    vt_ref[0] = t[:N_VT].astype(BF16)
    iw_ref[0] = t[N_VT:]


def _proj(x, g, w_nat, w_t):
    b, s, d = x.shape
    tm = min(512, s)
    return pl.pallas_call(
        _proj_kernel,
        grid=(b, s // tm),
        in_specs=[
            pl.BlockSpec((1, tm, d), lambda bi, i: (bi, i, 0)),
            pl.BlockSpec((1, d), lambda bi, i: (0, 0)),
            pl.BlockSpec((d, N_NAT), lambda bi, i: (0, 0)),
            pl.BlockSpec((N_T, d), lambda bi, i: (0, 0)),
        ],
        out_specs=[
            pl.BlockSpec((1, tm, N_MAIN), lambda bi, i: (bi, i, 0)),
            pl.BlockSpec((1, tm, 128), lambda bi, i: (bi, i, 0)),
            pl.BlockSpec((1, tm, BRANCH_W), lambda bi, i: (bi, i, 0)),
            pl.BlockSpec((1, N_VT, tm), lambda bi, i: (bi, 0, i)),
            pl.BlockSpec((1, 16, tm), lambda bi, i: (bi, 0, i)),
        ],
        out_shape=[
            jax.ShapeDtypeStruct((b, s, N_MAIN), BF16),
            jax.ShapeDtypeStruct((b, s, 128), BF16),
            jax.ShapeDtypeStruct((b, s, BRANCH_W), F32),
            jax.ShapeDtypeStruct((b, N_VT, s), BF16),
            jax.ShapeDtypeStruct((b, 16, s), F32),
        ],
        compiler_params=_params("arbitrary", "arbitrary"),
        name="norm_in_proj",
    )(x, g.reshape(1, d), w_nat, w_t)


def _pack_in_weights(w_in):
    d = w_in.shape[0]
    sizes = ([BRANCH_W] * 9 + [N_HEADS] + [BRANCH_W, HEAD_DIM, HEAD_DIM]
             + [IDX_HEADS * IDX_DIM, IDX_DIM, IDX_HEADS])
    offs = [0]
    for sz in sizes:
        offs.append(offs[-1] + sz)
    seg = [w_in[:, offs[i]:offs[i + 1]] for i in range(len(sizes))]
    (sb_q, sb_k, sb_v, mb_q, mb_k, mb_v, fx_q, fx_k, fx_v, fx_f,
     ds_q, ds_k, ds_v, ix_q, ix_k, ix_w) = seg
    scale = HEAD_DIM ** -0.5
    z = lambda n: jnp.zeros((d, n), w_in.dtype)
    w_nat = jnp.concatenate(
        [sb_q * scale, sb_k, mb_q * scale, mb_k, fx_q * scale, fx_k, ds_q * scale, ix_q,
         ds_k, ix_k, z(128 - HEAD_DIM - IDX_DIM)]
        + [w for h in range(N_HEADS) for w in [fx_f[:, h:h + 1]] * FOX_EXTRA + [z(HEAD_DIM - FOX_EXTRA)]],
        axis=1).astype(BF16)
    w_t = jnp.concatenate([sb_v, mb_v, fx_v, ds_v, ix_w, z(16 - IDX_HEADS)], axis=1).T.astype(BF16)
    return w_nat, w_t


CUM_BLOCK = 256


def _fox_c_kernel(f_ref, b_ref, pk_ref, pq_ref):
    s = f_ref.shape[1]
    row = lax.broadcasted_iota(I32, (CUM_BLOCK, CUM_BLOCK), 0)
    col = lax.broadcasted_iota(I32, (CUM_BLOCK, CUM_BLOCK), 1)
    tri = jnp.where(row >= col, 1.0, 0.0).astype(BF16)
    tri3 = jnp.concatenate([tri, tri, tri], axis=1)
    lane = lax.broadcasted_iota(I32, (CUM_BLOCK, BRANCH_W), 1) & (HEAD_DIM - 1)
    carry = jnp.zeros((1, BRANCH_W), F32)
    for blk in range(s // CUM_BLOCK):
        rows = slice(blk * CUM_BLOCK, (blk + 1) * CUM_BLOCK)
        f = f_ref[0, rows, :] + b_ref[...]
        ls = -_softplus(-f)
        hi = ls.astype(BF16)
        r1 = ls - hi.astype(F32)
        mid = r1.astype(BF16)
        lo = (r1 - mid.astype(F32)).astype(BF16)
        c = _dot(tri3, jnp.concatenate([hi, mid, lo], axis=0)) + carry
        carry = c[CUM_BLOCK - 1:CUM_BLOCK, :]
        c_hi = c.astype(BF16).astype(F32)
        r2 = c - c_hi
        c_mid = r2.astype(BF16).astype(F32)
        c_lo = r2 - c_mid
        third = jnp.where(lane >= 3, lane - 3, lane)
        piece = jnp.where(third == 0, c_hi, jnp.where(third == 1, c_mid, c_lo))
        pk = jnp.where(lane < 3, piece, jnp.where(lane < FOX_EXTRA, 1.0, 0.0))
        pq = jnp.where(lane < 3, -1.0, jnp.where(lane < FOX_EXTRA, piece, 0.0))
        pk_ref[0, rows, :] = pk.astype(BF16)
        pq_ref[0, rows, :] = pq.astype(BF16)


def _fox_c(fxf, fox_b):
    b, s, _ = fxf.shape
    bias = jnp.concatenate([v for h in range(N_HEADS)
                            for v in [fox_b[h:h + 1]] * FOX_EXTRA + [jnp.zeros((HEAD_DIM - FOX_EXTRA,), F32)]])
    spec = pl.BlockSpec((1, s, BRANCH_W), lambda bi: (bi, 0, 0))
    return pl.pallas_call(
        _fox_c_kernel,
        grid=(b,),
        in_specs=[spec, pl.BlockSpec((1, BRANCH_W), lambda bi: (0, 0))],
        out_specs=[spec, spec],
        out_shape=[jax.ShapeDtypeStruct((b, s, BRANCH_W), BF16)] * 2,
        compiler_params=_params("arbitrary"),
        name="fox_cumgate",
    )(fxf, bias.reshape(1, BRANCH_W))


def _tile_pos(j, jmax):
    return jnp.minimum(jmax - j, N_DIAG), pl.multiple_of(j * TK, TK)


def _tile_loop(n_tiles, body, init):
    def trip(t, carry):
        for u in range(Q_PER_K):
            carry = body(t * Q_PER_K + u, carry)
        return carry

    return lax.fori_loop(0, n_tiles // Q_PER_K, trip, init)


def _fold8(x, op):
    return op(x.reshape(TK // 8, 8, TQ), axis=0)


def _align_heads(dst_s, src, width=HEAD_DIM):
    for h in range(src.shape[1] // width):
        dst_s[:, h * 128:h * 128 + width] = src[:, h * width:(h + 1) * width]


def _head(h, width=HEAD_DIM):
    return slice(h * 128, h * 128 + width)


def _neg_rows():
    return tuple(jnp.full((8, TQ), -1e38, F32) for _ in range(N_HEADS))


def _weights_matmul(w_s, vt_ref, o_ref, acc_ref, n_tiles, inv_l, shared_v):
    def zero_body(j, carry):
        s0 = pl.multiple_of(j * TK, TK)
        for h in range(N_HEADS):
            w_s[h, pl.ds(s0, TK), :] = jnp.zeros((TK, TQ), BF16)
        return carry

    lax.fori_loop(n_tiles, w_s.shape[1] // TK, zero_body, 0)
    for h in range(N_HEADS):
        rows = slice(h * HEAD_DIM, (h + 1) * HEAD_DIM)
        out = _dot(vt_ref[0] if shared_v else vt_ref[0, rows, :], w_s[h])
        acc_ref[rows, :] = out if inv_l is None else out * inv_l[h]
    o_ref[0] = acc_ref[...].T.astype(o_ref.dtype)


def _softmax_passes(z_s, p_s, maxes, vt_ref, o_ref, acc_ref, n_tiles, shared_v=False):
    m = [jnp.max(mx, axis=0, keepdims=True) for mx in maxes]

    def exp_body(j, sums):
        s0 = pl.multiple_of(j * TK, TK)
        out = []
        for h in range(N_HEADS):
            p = jnp.exp2(z_s[h, pl.ds(s0, TK), :] - m[h])
            p_s[h, pl.ds(s0, TK), :] = p.astype(BF16)
            out.append(sums[h] + _fold8(p, jnp.sum))
        return tuple(out)

    sums = lax.fori_loop(0, n_tiles, exp_body, tuple(jnp.zeros((8, TQ), F32) for _ in range(N_HEADS)))
    inv_l = [1.0 / jnp.sum(sm, axis=0, keepdims=True) for sm in sums]
    _weights_matmul(p_s, vt_ref, o_ref, acc_ref, n_tiles, inv_l, shared_v)


def _sb_kernel(q_ref, k_ref, vt_ref, tb_ref, o_ref, acc_ref, base_s, hl_s, w_s):
    i = pl.program_id(1)
    jmax = (i + 1) * Q_PER_K - 1

    def score_body(j, carry):
        d, s0 = _tile_pos(j, jmax)
        sadd = tb_ref[0, d]
        smul = tb_ref[1, d]
        for h in range(N_HEADS):
            cols = slice(h * HEAD_DIM, (h + 1) * HEAD_DIM)
            z = _dot_nt(k_ref[0, pl.ds(s0, TK), cols], q_ref[0, :, cols])
            sp = jnp.maximum(z, 0.0) + jnp.log(1.0 + jnp.exp(-jnp.abs(z)))
            base_s[h, pl.ds(s0, TK), :] = z - sp + sadd
            lneg = sp * smul
            hi = lneg.astype(BF16)
            lo = (lneg - hi.astype(F32)).astype(BF16)
            hl_s[h, j] = jnp.concatenate([hi, lo], axis=0)
        return carry

    _tile_loop(jmax + 1, score_body, 0)

    row = lax.broadcasted_iota(I32, (TK + 16, 2 * TK), 0)
    col = lax.broadcasted_iota(I32, (TK + 16, 2 * TK), 1)
    upper = jnp.where(((col & (TK - 1)) > row) | (row >= TK), 1.0, 0.0).astype(BF16)

    def weight_body(jj, suffix):
        j = jmax - jj
        s0 = pl.multiple_of(j * TK, TK)
        out = []
        for h in range(N_HEADS):
            r = _dot(upper, hl_s[h, j])
            w = jnp.exp(base_s[h, pl.ds(s0, TK), :] + r[:TK] + suffix[h])
            w_s[h, pl.ds(s0, TK), :] = w.astype(BF16)
            out.append(suffix[h] + r[TK:TK + 1])
        return tuple(out)

    _tile_loop(jmax + 1, weight_body, tuple(jnp.zeros((1, TQ), F32) for _ in range(N_HEADS)))
    _weights_matmul(w_s, vt_ref, o_ref, acc_ref, jmax + 1, None, False)


def _moba_kernel(q_ref, k_ref, vt_ref, tb_ref, o_ref, acc_ref, z_s, p_s, selb_ref, km_s, qx_s, kx_s):
    i = pl.program_id(1)
    own = i
    jmax = (i + 1) * Q_PER_K - 1
    nb = k_ref.shape[1] // MOBA_BLOCK
    _align_heads(qx_s, q_ref[0])
    pl.when(i == 0)(lambda: _align_heads(kx_s, k_ref[0]))

    @pl.when(i == 0)
    def _():
        s_len = k_ref.shape[1]
        blk = lax.broadcasted_iota(I32, (8, s_len), 1) >> (MOBA_BLOCK.bit_length() - 1)
        avg = jnp.where(blk == lax.broadcasted_iota(I32, (8, s_len), 0), 1.0 / MOBA_BLOCK, 0.0)
        km_s[...] = _dot(avg.astype(BF16), k_ref[0])

    km_all = km_s[...]
    n_idx = lax.broadcasted_iota(I32, (8, TQ), 0)
    for h in range(N_HEADS):
        cols = slice(h * HEAD_DIM, (h + 1) * HEAD_DIM)
        q_h = q_ref[0, :, cols]
        km = km_all[:, cols]
        km_hi = km.astype(BF16)
        km_lo = (km - km_hi.astype(F32)).astype(BF16)
        gate = _dot_nt(km_hi, q_h) + _dot_nt(km_lo, q_h)
        rank = jnp.zeros((8, TQ), F32)
        for m in range(nb):
            g_m = gate[m:m + 1, :]
            wins = jnp.where(n_idx > m, jnp.where(g_m >= gate, 1.0, 0.0), jnp.where(g_m > gate, 1.0, 0.0))
            rank = rank + wins * jnp.where(m < own, 1.0, 0.0)
        chosen = jnp.where(rank < MOBA_TOPK, 0.0, NEG_BIG)
        selb_ref[h] = jnp.where(n_idx < own, chosen, jnp.where(n_idx == own, 0.0, NEG_BIG))

    def score_body(j, maxes):
        d, s0 = _tile_pos(j, jmax)
        n = j // (MOBA_BLOCK // TK)
        out = []
        for h in range(N_HEADS):
            z = _dot_nt(kx_s[pl.ds(s0, TK), _head(h)], qx_s[:, _head(h)])
            z = (z + tb_ref[h, d] + selb_ref[h, pl.ds(n, 1), :]) * LOG2E
            z_s[h, pl.ds(s0, TK), :] = z
            out.append(jnp.maximum(maxes[h], _fold8(z, jnp.max)))
        return tuple(out)

    maxes = _tile_loop(jmax + 1, score_body, _neg_rows())
    _softmax_passes(z_s, p_s, maxes, vt_ref, o_ref, acc_ref, jmax + 1)


def _fox_kernel(q_ref, k_ref, pq_ref, pk_ref, vt_ref, tb_ref, o_ref, acc_ref, z_s, p_s, qx_s, kx_s):
    i = pl.program_id(1)
    jmax = (i + 1) * Q_PER_K - 1

    for h in range(N_HEADS):
        cols = slice(h * HEAD_DIM, (h + 1) * HEAD_DIM)
        qx_s[:, h * 128:h * 128 + HEAD_DIM] = q_ref[0, :, cols]
        qx_s[:, h * 128 + HEAD_DIM:(h + 1) * 128] = pq_ref[0, :, cols]

    @pl.when(i == 0)
    def _():
        for h in range(N_HEADS):
            cols = slice(h * HEAD_DIM, (h + 1) * HEAD_DIM)
            kx_s[:, h * 128:h * 128 + HEAD_DIM] = k_ref[0, :, cols]
            kx_s[:, h * 128 + HEAD_DIM:(h + 1) * 128] = pk_ref[0, :, cols]

    def score_body(j, maxes):
        d, s0 = _tile_pos(j, jmax)
        causal = tb_ref[0, d]
        out = []
        for h in range(N_HEADS):
            ext = slice(h * 128, (h + 1) * 128)
            z = (_dot_nt(kx_s[pl.ds(s0, TK), ext], qx_s[:, ext]) + causal) * LOG2E
            z_s[h, pl.ds(s0, TK), :] = z
            out.append(jnp.maximum(maxes[h], _fold8(z, jnp.max)))
        return tuple(out)

    maxes = _tile_loop(jmax + 1, score_body, _neg_rows())
    _softmax_passes(z_s, p_s, maxes, vt_ref, o_ref, acc_ref, jmax + 1)


I16 = jnp.int16
HALF = 1 << 15


def _count16(ref, n_tiles, thr16, strict):
    def body(j, cnt):
        t = ref[j]
        hit = (t > thr16) if strict else (t >= thr16)
        g = jnp.where(hit, jnp.int16(1), jnp.int16(0)).reshape(TK // 16, 16, TQ)
        for r in range(TK // 16):
            cnt = cnt + g[r]
        return cnt

    cnt = _tile_loop(n_tiles, body, jnp.zeros((16, TQ), I16))
    return jnp.sum(cnt.astype(I32), axis=0, keepdims=True)


def _radix16(ref, n_tiles, need):
    def bit_body(it, t_u):
        cand = t_u | (jnp.int32(1) << (15 - it))
        cnt = _count16(ref, n_tiles, (cand - HALF).astype(I16), False)
        return jnp.where(cnt >= need, cand, t_u)

    return lax.fori_loop(0, 16, bit_body, jnp.zeros((1, TQ), I32))


def _dsa_kernel(q_ref, iq_ref, kv_ref, vt_ref, iw_ref, tb_ref, tc_ref, o_ref,
                acc_ref, z_s, p_s, keys_ref, khi_ref, klo_ref, qx_s, iqx_s, *, topk):
    i = pl.program_id(1)
    jmax = (i + 1) * Q_PER_K - 1
    _align_heads(qx_s, q_ref[0])
    _align_heads(iqx_s, iq_ref[0], IDX_DIM)

    def index_body(j, carry):
        d, s0 = _tile_pos(j, jmax)
        k_idx = kv_ref[0, pl.ds(s0, TK), HEAD_DIM:HEAD_DIM + IDX_DIM]
        sc = jnp.zeros((TK, TQ), F32)
        for hh in range(IDX_HEADS):
            qi = iqx_s[:, _head(hh, IDX_DIM)]
            sc = sc + iw_ref[0, hh:hh + 1, :] * jnp.maximum(_dot_nt(k_idx, qi), 0.0)
        sc = jnp.where(sc == 0.0, 0.0, sc)
        bits = lax.bitcast_convert_type(sc, I32)
        key = bits ^ ((bits >> 31) & 0x7FFFFFFF)
        key = jnp.where(tc_ref[0, d] < 0.0, INT_MIN, key)
        keys_ref[j] = key
        khi_ref[j] = (key >> 16).astype(I16)
        klo_ref[j] = ((key & 0xFFFF) - HALF).astype(I16)
        return carry

    _tile_loop(jmax + 1, index_body, 0)

    n_tiles = jmax + 1
    t_hi = _radix16(khi_ref, n_tiles, topk)
    t_hi16 = (t_hi - HALF).astype(I16)
    need_lo = topk - _count16(khi_ref, n_tiles, t_hi16, True)

    def restrict_body(j, carry):
        klo_ref[j] = jnp.where(khi_ref[j] == t_hi16, klo_ref[j], jnp.int16(-HALF))
        return carry

    _tile_loop(n_tiles, restrict_body, 0)
    t_lo = _radix16(klo_ref, n_tiles, need_lo)
    thr = (t_hi - HALF) * (1 << 16) + t_lo
    n_above = (topk - need_lo) + _count16(klo_ref, n_tiles, (t_lo - HALF).astype(I16), True)
    need_eq = (topk - n_above).astype(F32)

    row = lax.broadcasted_iota(I32, (TK + 16, TK), 0)
    col = lax.broadcasted_iota(I32, (TK + 16, TK), 1)
    lower = jnp.where((col < row) | (row >= TK), 1.0, 0.0).astype(BF16)

    def score_body(j, carry):
        maxes, seen = carry
        d, s0 = _tile_pos(j, jmax)
        k_j = kv_ref[0, pl.ds(s0, TK), 0:HEAD_DIM]
        key = keys_ref[j]
        is_thr = key == thr
        r = _dot(lower, jnp.where(is_thr, 1.0, 0.0).astype(BF16))
        tie = jnp.where(r[:TK] + seen < need_eq, 0.0, NEG_BIG)
        selb = jnp.where(key > thr, 0.0, jnp.where(is_thr, tie, NEG_BIG))
        out = []
        for h in range(N_HEADS):
            z = (_dot_nt(k_j, qx_s[:, _head(h)]) + tb_ref[h, d] + selb) * LOG2E
            z_s[h, pl.ds(s0, TK), :] = z
            out.append(jnp.maximum(maxes[h], _fold8(z, jnp.max)))
        return tuple(out), seen + r[TK:TK + 1]

    maxes, _ = _tile_loop(jmax + 1, score_body, (_neg_rows(), jnp.zeros((1, TQ), F32)))
    _softmax_passes(z_s, p_s, maxes, vt_ref, o_ref, acc_ref, jmax + 1, shared_v=True)


def _mixer_call(kernel_fn, name, b, s, in_specs, scratch, args):
    return pl.pallas_call(
        kernel_fn,
        grid=(b, s // TQ),
        in_specs=in_specs,
        out_specs=pl.BlockSpec((1, TQ, BRANCH_W), lambda bi, i: (bi, i, 0)),
        out_shape=jax.ShapeDtypeStruct((b, s, BRANCH_W), BF16),
        scratch_shapes=scratch,
        compiler_params=_params("arbitrary", "arbitrary"),
        name=name,
    )(*args)


def _q_spec(width, col):
    return pl.BlockSpec((1, TQ, width), lambda bi, i: (bi, i, col))


def _k_spec(s, width, col):
    return pl.BlockSpec((1, s, width), lambda bi, i: (bi, 0, col))


def _vt_spec(s, rows, row_blk):
    return pl.BlockSpec((1, rows, s), lambda bi, i: (bi, row_blk, 0))


def _tile_spec(n, blk):
    return pl.BlockSpec((n, N_DIAG + 1, TK, TQ), lambda bi, i: (blk, 0, 0, 0))


_ACC = lambda: pltpu.VMEM((BRANCH_W, TQ), F32)
_SCORES = lambda s: pltpu.VMEM((N_HEADS, s, TQ), F32)
_WEIGHTS = lambda s: pltpu.VMEM((N_HEADS, s, TQ), BF16)
_HEADS128 = lambda rows: pltpu.VMEM((rows, N_HEADS * 128), BF16)


def _sb(main, vt, tiles):
    b, s, _ = main.shape
    return _mixer_call(
        _sb_kernel, "sb_mixer", b, s,
        [_q_spec(BRANCH_W, 0), _k_spec(s, BRANCH_W, 1), _vt_spec(s, BRANCH_W, 0),
         _tile_spec(2, KIND_STRICT_ADD // 2)],
        [_ACC(), _SCORES(s), pltpu.VMEM((N_HEADS, s // TK, 2 * TK, TQ), BF16), _WEIGHTS(s)],
        (main, main, vt, tiles))


def _moba(main, vt, tiles):
    b, s, _ = main.shape
    return _mixer_call(
        _moba_kernel, "moba_mixer", b, s,
        [_q_spec(BRANCH_W, 2), _k_spec(s, BRANCH_W, 3), _vt_spec(s, BRANCH_W, 1),
         _tile_spec(N_HEADS, 0)],
        [_ACC(), _SCORES(s), _WEIGHTS(s), pltpu.VMEM((N_HEADS, 8, TQ), F32),
         pltpu.VMEM((8, BRANCH_W), F32), _HEADS128(TQ), _HEADS128(s)], (main, main, vt, tiles))


def _fox(main, pq, pk, vt, tiles):
    b, s, _ = main.shape
    return _mixer_call(
        _fox_kernel, "fox_mixer", b, s,
        [_q_spec(BRANCH_W, 4), _k_spec(s, BRANCH_W, 5), _q_spec(BRANCH_W, 0), _k_spec(s, BRANCH_W, 0),
         _vt_spec(s, BRANCH_W, 2), _tile_spec(1, KIND_CAUSAL)],
        [_ACC(), _SCORES(s), _WEIGHTS(s), _HEADS128(TQ), _HEADS128(s)],
        (main, main, pq, pk, vt, tiles))


def _dsa(main, kv, vt, iw, tiles):
    b, s, _ = main.shape
    topk = min(DSA_TOPK, s // 4)
    return _mixer_call(
        functools.partial(_dsa_kernel, topk=topk), "dsa_mixer", b, s,
        [_q_spec(BRANCH_W, 6), _q_spec(BRANCH_W, 7), _k_spec(s, 128, 0),
         _vt_spec(s, HEAD_DIM, 3 * BRANCH_W // HEAD_DIM),
         pl.BlockSpec((1, 16, TQ), lambda bi, i: (bi, 0, i)),
         _tile_spec(N_HEADS, 1), _tile_spec(1, KIND_CAUSAL)],
        [_ACC(), _SCORES(s), _WEIGHTS(s), pltpu.VMEM((s // TK, TK, TQ), I32),
         pltpu.VMEM((s // TK, TK, TQ), I16), pltpu.VMEM((s // TK, TK, TQ), I16), _HEADS128(TQ),
         pltpu.VMEM((TQ, IDX_HEADS * 128), BF16)],
        (main, main, kv, vt, iw, tiles, tiles))


def _combine_kernel(x_ref, g_ref, o0, o1, o2, o3, wg_ref, wb_ref, wo_ref, out_ref):
    x = x_ref[...]
    xn = _rmsnorm(x, g_ref[...]).astype(BF16)
    mixed = None
    for i, o_ref in enumerate((o0, o1, o2, o3)):
        gate = 1.0 / (1.0 + jnp.exp(-_dot(xn, wg_ref[i])))
        y = gate * _dot(o_ref[...], wb_ref[i])
        mixed = y if mixed is None else mixed + y
    out_ref[...] = x + _dot(mixed.astype(BF16), wo_ref[...])


def _combine(x2, g, outs, wg, wb, wo):
    t, d = x2.shape
    tm = 256
    row = lambda i: (i, 0)
    return pl.pallas_call(
        _combine_kernel,
        grid=(t // tm,),
        in_specs=[pl.BlockSpec((tm, d), row), pl.BlockSpec((1, d), lambda i: (0, 0))]
        + [pl.BlockSpec((tm, BRANCH_W), row)] * N_BRANCH
        + [pl.BlockSpec((N_BRANCH, d, d), lambda i: (0, 0, 0)),
           pl.BlockSpec((N_BRANCH, BRANCH_W, d), lambda i: (0, 0, 0)),
           pl.BlockSpec((d, d), lambda i: (0, 0))],
        out_specs=pl.BlockSpec((tm, d), row),
        out_shape=jax.ShapeDtypeStruct((t, d), F32),
        compiler_params=_params("arbitrary"),
        name="gate_combine_out",
    )(x2, g.reshape(1, d), *outs, wg, wb, wo)


FFN_CHUNK = 256
HALO = 16


def _gelu(g):
    return 0.5 * g * (1.0 + lax.erf(g * (2.0 ** -0.5)))


def _ffn_kernel(x_ref, xp_ref, g_ref, gf_ref, wg_ref, wu_ref, cw_ref, wd_ref, o_ref, xn_s, hg_s, hu_s, *,
                final_norm):
    i = pl.program_id(1)
    tm = x_ref.shape[1]
    x = x_ref[0]
    gain = g_ref[...]
    prev = _rmsnorm(xp_ref[0], gain) * jnp.where(i > 0, 1.0, 0.0)
    xn_s[...] = jnp.concatenate([jnp.zeros_like(prev), prev, _rmsnorm(x, gain)], axis=0).astype(BF16)

    def taps(h_s, cw, r):
        out = cw[r + 3:r + 4, :]
        for j in range(CONV_W):
            out = out + cw[r + j:r + j + 1, :] * h_s[HALO - (CONV_W - 1) + j:HALO - (CONV_W - 1) + j + tm, :]
        return out

    nc = wg_ref.shape[0]

    def up(c):
        xn = xn_s[...]
        hg_s[c % 2] = _dot(xn, wg_ref[c])
        hu_s[c % 2] = _dot(xn, wu_ref[c])

    up(0)
    acc = x
    for c in range(nc):
        if c + 1 < nc:
            up(c + 1)
        cw = cw_ref[c]
        a = _gelu(taps(hg_s.at[c % 2], cw, 0)) * taps(hu_s.at[c % 2], cw, 4)
        acc = acc + _dot(a.astype(BF16), wd_ref[c])
    o_ref[0] = _rmsnorm(acc, gf_ref[...]) if final_norm else acc


def _ffn(x, g, g_final, final_norm, wg, wu, cw, wd):
    b, s, d = x.shape
    tm = 256
    nc = wg.shape[0]
    const3 = lambda bi, i: (0, 0, 0)
    return pl.pallas_call(
        functools.partial(_ffn_kernel, final_norm=final_norm),
        grid=(b, s // tm),
        in_specs=[
            pl.BlockSpec((1, tm, d), lambda bi, i: (bi, i, 0)),
            pl.BlockSpec((1, 8, d), lambda bi, i: (bi, jnp.maximum(i * (tm // 8) - 1, 0), 0)),
            pl.BlockSpec((1, d), lambda bi, i: (0, 0)),
            pl.BlockSpec((1, d), lambda bi, i: (0, 0)),
            pl.BlockSpec((nc, d, FFN_CHUNK), const3),
            pl.BlockSpec((nc, d, FFN_CHUNK), const3),
            pl.BlockSpec((nc, 8, FFN_CHUNK), const3),
            pl.BlockSpec((nc, FFN_CHUNK, d), const3),
        ],
        out_specs=pl.BlockSpec((1, tm, d), lambda bi, i: (bi, i, 0)),
        out_shape=jax.ShapeDtypeStruct((b, s, d), F32),
        scratch_shapes=[pltpu.VMEM((tm + HALO, d), BF16),
                        pltpu.VMEM((2, tm + HALO, FFN_CHUNK), F32),
                        pltpu.VMEM((2, tm + HALO, FFN_CHUNK), F32)],
        compiler_params=_params("arbitrary", "arbitrary"),
        name="conv_ffn",
    )(x, x, g.reshape(1, d), g_final.reshape(1, d), wg, wu, cw, wd)


def _pack_ffn_weights(w_up, conv_w, conv_b, w_down):
    d, two_f = w_up.shape
    f = two_f // 2
    nc = f // FFN_CHUNK
    chunks = lambda w: w.reshape(w.shape[0], nc, FFN_CHUNK).transpose(1, 0, 2)
    wg = chunks(w_up[:, :f]).astype(BF16)
    wu = chunks(w_up[:, f:]).astype(BF16)
    cw = jnp.concatenate([conv_w[:, :f], conv_b[None, :f], conv_w[:, f:], conv_b[None, f:]], axis=0)
    wd = w_down.reshape(nc, FFN_CHUNK, d).astype(BF16)
    return wg, wu, chunks(cw), wd


def _token_mixers(x, g, w_in, fox_b, tiles):
    w_nat, w_t = _pack_in_weights(w_in)
    main, kv, fxf, vt, iw = _proj(x, g, w_nat, w_t)
    pk, pq = _fox_c(fxf, fox_b)
    return (_sb(main, vt, tiles), _moba(main, vt, tiles), _fox(main, pq, pk, vt, tiles),
            _dsa(main, kv, vt, iw, tiles))


def kernel(x, norm_mix_g, norm_ffn_g, norm_final_g, w_in, fox_b_f, t5_bias, w_gate, w_branch, w_out,
           w_up, conv_w, conv_b, w_down):
    b, s, d = x.shape
    assert s % TQ == 0 and TQ == MOBA_BLOCK and s // MOBA_BLOCK <= 8
    tiles = _make_tiles(t5_bias)
    depth = w_in.shape[0]
    for l in range(depth):
        outs = _token_mixers(x, norm_mix_g[l], w_in[l], fox_b_f[l], tiles)
        outs = [o.reshape(b * s, BRANCH_W) for o in outs]
        x = _combine(x.reshape(b * s, d), norm_mix_g[l], outs, w_gate[l].astype(BF16),
                     w_branch[l].astype(BF16), w_out[l].astype(BF16)).reshape(b, s, d)
        x = _ffn(x, norm_ffn_g[l], norm_final_g, l == depth - 1,
                 *_pack_ffn_weights(w_up[l], conv_w[l], conv_b[l], w_down[l]))
    return x
```

```python
import functools
import math

import jax
import jax.numpy as jnp
from jax import lax
from jax.experimental import pallas as pl
from jax.experimental.pallas import tpu as pltpu

F32 = jnp.float32
BF16 = jnp.bfloat16
I32 = jnp.int32

HEAD_DIM = 64
N_HEADS = 4
BRANCH_W = N_HEADS * HEAD_DIM
N_BRANCH = 4
MOBA_BLOCK = 256
MOBA_TOPK = 3
DSA_TOPK = 256
IDX_HEADS = 8
IDX_DIM = 32
T5_BUCKETS = 32
T5_MAX_DIST = 128
CONV_W = 3
RMS_EPS = 1e-6
NEG_BIG = -1e30
INT_MIN = -2147483648
LOG2E = math.log2(math.e)

TQ = 256
TK = 128
Q_PER_K = TQ // TK
T5_CONST_DIST = 113
N_DIAG = (TQ + T5_CONST_DIST - 2) // TK + 1
FAR_DIST = 1 << 20

KIND_CAUSAL = 8
KIND_STRICT_ADD = 10
KIND_STRICT_MUL = 11
N_KINDS = 12

VMEM_LIMIT = 56 * 1024 * 1024


def _dot(a, b):
    return jnp.dot(a, b, preferred_element_type=F32)


def _dot_nt(a, b):
    return lax.dot_general(a, b, (((1,), (1,)), ((), ())), preferred_element_type=F32)


def _rmsnorm(x, g):
    return x * lax.rsqrt(jnp.mean(x * x, axis=-1, keepdims=True) + RMS_EPS) * g


def _softplus(z):
    return jnp.maximum(z, 0.0) + jnp.log1p(jnp.exp(-jnp.abs(z)))


def _params(*sem):
    return pltpu.CompilerParams(dimension_semantics=sem, vmem_limit_bytes=VMEM_LIMIT)


def _tiles_kernel(tab_ref, out_ref):
    kind = pl.program_id(0)
    d = pl.program_id(1)
    sl = lax.broadcasted_iota(I32, (TK, TQ), 0)
    tl = lax.broadcasted_iota(I32, (TK, TQ), 1)
    dist = jnp.where(d == N_DIAG, FAR_DIST, (TK - TQ) + d * TK + tl - sl)
    n = jnp.maximum(dist, 0)
    max_exact = T5_BUCKETS // 2
    nf = jnp.maximum(n, 1).astype(F32)
    large = max_exact + (jnp.log(nf / max_exact) / math.log(T5_MAX_DIST / max_exact)
                         * (T5_BUCKETS - max_exact)).astype(I32)
    large = jnp.minimum(large, T5_BUCKETS - 1)
    bucket = jnp.where(n < max_exact, n, large)
    h = jnp.minimum(kind, 2 * N_HEADS - 1)
    bias = jnp.zeros((TK, TQ), F32)
    for b in range(T5_BUCKETS):
        bias = jnp.where(bucket == b, tab_ref[b, h], bias)
    causal = dist >= 0
    strict = dist > 0
    v_bias = jnp.where(causal, bias, NEG_BIG)
    v_causal = jnp.where(causal, 0.0, NEG_BIG)
    v_sadd = jnp.where(strict, 0.0, NEG_BIG)
    v_smul = jnp.where(strict, 1.0, 0.0)
    out_ref[0, 0] = jnp.where(kind < KIND_CAUSAL, v_bias,
                              jnp.where(kind < KIND_STRICT_ADD, v_causal,
                                        jnp.where(kind == KIND_STRICT_ADD, v_sadd, v_smul)))


def _make_tiles(t5_bias):
    return pl.pallas_call(
        _tiles_kernel,
        grid=(N_KINDS, N_DIAG + 1),
        in_specs=[pl.BlockSpec(memory_space=pltpu.SMEM)],
        out_specs=pl.BlockSpec((1, 1, TK, TQ), lambda k, d: (k, d, 0, 0)),
        out_shape=jax.ShapeDtypeStruct((N_KINDS, N_DIAG + 1, TK, TQ), F32),
        compiler_params=_params("arbitrary", "arbitrary"),
        name="t5_tiles",
    )(t5_bias)


N_MAIN = 8 * BRANCH_W
FOX_EXTRA = 6
N_NAT = N_MAIN + 128 + BRANCH_W
N_VT = 3 * BRANCH_W + HEAD_DIM
N_T = N_VT + 16
PROJ_CHUNK = 512


def _proj_kernel(x_ref, g_ref, w_ref, wt_ref, main_ref, kv_ref, fxf_ref, vt_ref, iw_ref):
    xn = _rmsnorm(x_ref[0], g_ref[...]).astype(BF16)
    for c in range(0, N_MAIN, PROJ_CHUNK):
        main_ref[0, :, c:c + PROJ_CHUNK] = _dot(xn, w_ref[:, c:c + PROJ_CHUNK]).astype(BF16)
    r = _dot(xn, w_ref[:, N_MAIN:N_NAT])
    kv_ref[0] = r[:, :128].astype(BF16)
    fxf_ref[0] = r[:, 128:]
    t = _dot_nt(wt_ref[...], xn)
    vt_ref[0] = t[:N_VT].astype(BF16)
    iw_ref[0] = t[N_VT:]


def _proj(x, g, w_nat, w_t):
    b, s, d = x.shape
    tm = min(512, s)
    return pl.pallas_call(
        _proj_kernel,
        grid=(b, s // tm),
        in_specs=[
            pl.BlockSpec((1, tm, d), lambda bi, i: (bi, i, 0)),
            pl.BlockSpec((1, d), lambda bi, i: (0, 0)),
            pl.BlockSpec((d, N_NAT), lambda bi, i: (0, 0)),
            pl.BlockSpec((N_T, d), lambda bi, i: (0, 0)),
        ],
        out_specs=[
            pl.BlockSpec((1, tm, N_MAIN), lambda bi, i: (bi, i, 0)),
            pl.BlockSpec((1, tm, 128), lambda bi, i: (bi, i, 0)),
            pl.BlockSpec((1, tm, BRANCH_W), lambda bi, i: (bi, i, 0)),
            pl.BlockSpec((1, N_VT, tm), lambda bi, i: (bi, 0, i)),
            pl.BlockSpec((1, 16, tm), lambda bi, i: (bi, 0, i)),
        ],
        out_shape=[
            jax.ShapeDtypeStruct((b, s, N_MAIN), BF16),
            jax.ShapeDtypeStruct((b, s, 128), BF16),
            jax.ShapeDtypeStruct((b, s, BRANCH_W), F32),
            jax.ShapeDtypeStruct((b, N_VT, s), BF16),
            jax.ShapeDtypeStruct((b, 16, s), F32),
        ],
        compiler_params=_params("arbitrary", "arbitrary"),
        name="norm_in_proj",
    )(x, g.reshape(1, d), w_nat, w_t)


def _pack_in_weights(w_in):
    d = w_in.shape[0]
    sizes = ([BRANCH_W] * 9 + [N_HEADS] + [BRANCH_W, HEAD_DIM, HEAD_DIM]
             + [IDX_HEADS * IDX_DIM, IDX_DIM, IDX_HEADS])
    offs = [0]
    for sz in sizes:
        offs.append(offs[-1] + sz)
    seg = [w_in[:, offs[i]:offs[i + 1]] for i in range(len(sizes))]
    (sb_q, sb_k, sb_v, mb_q, mb_k, mb_v, fx_q, fx_k, fx_v, fx_f,
     ds_q, ds_k, ds_v, ix_q, ix_k, ix_w) = seg
    scale = HEAD_DIM ** -0.5
    z = lambda n: jnp.zeros((d, n), w_in.dtype)
    w_nat = jnp.concatenate(
        [sb_q * scale, sb_k, mb_q * scale, mb_k, fx_q * scale, fx_k, ds_q * scale, ix_q,
         ds_k, ix_k, z(128 - HEAD_DIM - IDX_DIM)]
        + [w for h in range(N_HEADS) for w in [fx_f[:, h:h + 1]] * FOX_EXTRA + [z(HEAD_DIM - FOX_EXTRA)]],
        axis=1).astype(BF16)
    w_t = jnp.concatenate([sb_v, mb_v, fx_v, ds_v, ix_w, z(16 - IDX_HEADS)], axis=1).T.astype(BF16)
    return w_nat, w_t


CUM_BLOCK = 256


def _fox_c_kernel(f_ref, b_ref, pk_ref, pq_ref):
    s = f_ref.shape[1]
    row = lax.broadcasted_iota(I32, (CUM_BLOCK, CUM_BLOCK), 0)
    col = lax.broadcasted_iota(I32, (CUM_BLOCK, CUM_BLOCK), 1)
    tri = jnp.where(row >= col, 1.0, 0.0).astype(BF16)
    tri3 = jnp.concatenate([tri, tri, tri], axis=1)
    lane = lax.broadcasted_iota(I32, (CUM_BLOCK, BRANCH_W), 1) & (HEAD_DIM - 1)
    carry = jnp.zeros((1, BRANCH_W), F32)
    for blk in range(s // CUM_BLOCK):
        rows = slice(blk * CUM_BLOCK, (blk + 1) * CUM_BLOCK)
        f = f_ref[0, rows, :] + b_ref[...]
        ls = -_softplus(-f)
        hi = ls.astype(BF16)
        r1 = ls - hi.astype(F32)
        mid = r1.astype(BF16)
        lo = (r1 - mid.astype(F32)).astype(BF16)
        c = _dot(tri3, jnp.concatenate([hi, mid, lo], axis=0)) + carry
        carry = c[CUM_BLOCK - 1:CUM_BLOCK, :]
        c_hi = c.astype(BF16).astype(F32)
        r2 = c - c_hi
        c_mid = r2.astype(BF16).astype(F32)
        c_lo = r2 - c_mid
        third = jnp.where(lane >= 3, lane - 3, lane)
        piece = jnp.where(third == 0, c_hi, jnp.where(third == 1, c_mid, c_lo))
        pk = jnp.where(lane < 3, piece, jnp.where(lane < FOX_EXTRA, 1.0, 0.0))
        pq = jnp.where(lane < 3, -1.0, jnp.where(lane < FOX_EXTRA, piece, 0.0))
        pk_ref[0, rows, :] = pk.astype(BF16)
        pq_ref[0, rows, :] = pq.astype(BF16)


def _fox_c(fxf, fox_b):
    b, s, _ = fxf.shape
    bias = jnp.concatenate([v for h in range(N_HEADS)
                            for v in [fox_b[h:h + 1]] * FOX_EXTRA + [jnp.zeros((HEAD_DIM - FOX_EXTRA,), F32)]])
    spec = pl.BlockSpec((1, s, BRANCH_W), lambda bi: (bi, 0, 0))
    return pl.pallas_call(
        _fox_c_kernel,
        grid=(b,),
        in_specs=[spec, pl.BlockSpec((1, BRANCH_W), lambda bi: (0, 0))],
        out_specs=[spec, spec],
        out_shape=[jax.ShapeDtypeStruct((b, s, BRANCH_W), BF16)] * 2,
        compiler_params=_params("arbitrary"),
        name="fox_cumgate",
    )(fxf, bias.reshape(1, BRANCH_W))


def _tile_pos(j, jmax):
    return jnp.minimum(jmax - j, N_DIAG), pl.multiple_of(j * TK, TK)


def _tile_loop(n_tiles, body, init, first=0):
    def trip(t, carry):
        for u in range(Q_PER_K):
            carry = body(t * Q_PER_K + u, carry)
        return carry

    return lax.fori_loop(first // Q_PER_K, n_tiles // Q_PER_K, trip, init)


def _fold8(x, op):
    return op(x.reshape(TK // 8, 8, TQ), axis=0)


def _align_heads(dst_s, src, width=HEAD_DIM):
    for h in range(src.shape[1] // width):
        dst_s[:, h * 128:h * 128 + width] = src[:, h * width:(h + 1) * width]


def _head(h, width=HEAD_DIM):
    return slice(h * 128, h * 128 + width)


def _neg_rows():
    return tuple(jnp.full((8, TQ), -1e38, F32) for _ in range(N_HEADS))


def _weights_matmul(w_s, vt_ref, o_ref, acc_ref, n_tiles, inv_l, shared_v):
    s = w_s.shape[1]
    half = s // 2
    use_second = n_tiles * TK > half

    def zero_body(j, carry):
        s0 = pl.multiple_of(j * TK, TK)
        for h in range(N_HEADS):
            w_s[h, pl.ds(s0, TK), :] = jnp.zeros((TK, TQ), BF16)
        return carry

    lax.fori_loop(n_tiles, jnp.where(use_second, s // TK, half // TK), zero_body, 0)

    def product(lo, first):
        for h in range(N_HEADS):
            rows = slice(h * HEAD_DIM, (h + 1) * HEAD_DIM)
            vt = vt_ref[0, :, lo:lo + half] if shared_v else vt_ref[0, rows, lo:lo + half]
            out = _dot(vt, w_s[h, lo:lo + half, :])
            acc_ref[rows, :] = out if first else acc_ref[rows, :] + out

    product(0, True)
    pl.when(use_second)(functools.partial(product, half, False))
    if inv_l is not None:
        for h in range(N_HEADS):
            rows = slice(h * HEAD_DIM, (h + 1) * HEAD_DIM)
            acc_ref[rows, :] = acc_ref[rows, :] * inv_l[h]
    o_ref[0] = acc_ref[...].T.astype(o_ref.dtype)


def _softmax_passes(z_s, p_s, maxes, vt_ref, o_ref, acc_ref, n_tiles, shared_v=False):
    m = [jnp.max(mx, axis=0, keepdims=True) for mx in maxes]

    def exp_body(j, sums):
        s0 = pl.multiple_of(j * TK, TK)
        out = []
        for h in range(N_HEADS):
            p = jnp.exp2(z_s[h, pl.ds(s0, TK), :] - m[h])
            p_s[h, pl.ds(s0, TK), :] = p.astype(BF16)
            out.append(sums[h] + _fold8(p, jnp.sum))
        return tuple(out)

    sums = lax.fori_loop(0, n_tiles, exp_body, tuple(jnp.zeros((8, TQ), F32) for _ in range(N_HEADS)))
    inv_l = [1.0 / jnp.sum(sm, axis=0, keepdims=True) for sm in sums]
    _weights_matmul(p_s, vt_ref, o_ref, acc_ref, n_tiles, inv_l, shared_v)


def _sb_kernel(q_ref, k_ref, vt_ref, tb_ref, o_ref, acc_ref, base_s, hl_s, w_s):
    i = pl.program_id(1)
    jmax = (i + 1) * Q_PER_K - 1

    def score_body(masked, j, carry):
        d, s0 = _tile_pos(j, jmax)
        for h in range(N_HEADS):
            cols = slice(h * HEAD_DIM, (h + 1) * HEAD_DIM)
            z = _dot_nt(k_ref[0, pl.ds(s0, TK), cols], q_ref[0, :, cols])
            sp = jnp.maximum(z, 0.0) + jnp.log(1.0 + jnp.exp(-jnp.abs(z)))
            logsig = z - sp
            base_s[h, pl.ds(s0, TK), :] = logsig + tb_ref[0, d] if masked else logsig
            pos = sp * tb_ref[1, d] if masked else sp
            hi = pos.astype(BF16)
            lo = (pos - hi.astype(F32)).astype(BF16)
            hl_s[h, j] = jnp.concatenate([hi, lo], axis=0)
        return carry

    n_tiles = jmax + 1
    n_plain = jnp.maximum(n_tiles - Q_PER_K * ((N_DIAG + Q_PER_K - 1) // Q_PER_K), 0)
    _tile_loop(n_plain, functools.partial(score_body, False), 0)
    _tile_loop(n_tiles, functools.partial(score_body, True), 0, first=n_plain)

    row = lax.broadcasted_iota(I32, (TK + 16, 2 * TK), 0)
    col = lax.broadcasted_iota(I32, (TK + 16, 2 * TK), 1)
    upper = jnp.where(((col & (TK - 1)) > row) | (row >= TK), -1.0, 0.0).astype(BF16)

    def weight_body(jj, suffix):
        j = jmax - jj
        s0 = pl.multiple_of(j * TK, TK)
        out = []
        for h in range(N_HEADS):
            r = _dot(upper, hl_s[h, j])
            w = jnp.exp(base_s[h, pl.ds(s0, TK), :] + r[:TK] + suffix[h])
            w_s[h, pl.ds(s0, TK), :] = w.astype(BF16)
            out.append(suffix[h] + r[TK:TK + 1])
        return tuple(out)

    _tile_loop(jmax + 1, weight_body, tuple(jnp.zeros((1, TQ), F32) for _ in range(N_HEADS)))
    _weights_matmul(w_s, vt_ref, o_ref, acc_ref, jmax + 1, None, False)


def _moba_kernel(q_ref, k_ref, vt_ref, tb_ref, o_ref, acc_ref, z_s, p_s, selb_ref, km_s, qx_s, kx_s):
    i = pl.program_id(1)
    own = i
    jmax = (i + 1) * Q_PER_K - 1
    nb = k_ref.shape[1] // MOBA_BLOCK
    _align_heads(qx_s, q_ref[0])
    pl.when(i == 0)(lambda: _align_heads(kx_s, k_ref[0]))

    @pl.when(i == 0)
    def _():
        s_len = k_ref.shape[1]
        blk = lax.broadcasted_iota(I32, (8, s_len), 1) >> (MOBA_BLOCK.bit_length() - 1)
        avg = jnp.where(blk == lax.broadcasted_iota(I32, (8, s_len), 0), 1.0 / MOBA_BLOCK, 0.0)
        km_s[...] = _dot(avg.astype(BF16), k_ref[0])

    km_all = km_s[...]
    n_idx = lax.broadcasted_iota(I32, (8, TQ), 0)
    for h in range(N_HEADS):
        cols = slice(h * HEAD_DIM, (h + 1) * HEAD_DIM)
        q_h = q_ref[0, :, cols]
        km = km_all[:, cols]
        km_hi = km.astype(BF16)
        km_lo = (km - km_hi.astype(F32)).astype(BF16)
        gate = _dot_nt(km_hi, q_h) + _dot_nt(km_lo, q_h)
        rank = jnp.zeros((8, TQ), F32)
        for m in range(nb):
            g_m = gate[m:m + 1, :]
            wins = jnp.where(n_idx > m, jnp.where(g_m >= gate, 1.0, 0.0), jnp.where(g_m > gate, 1.0, 0.0))
            rank = rank + wins * jnp.where(m < own, 1.0, 0.0)
        chosen = jnp.where(rank < MOBA_TOPK, 0.0, NEG_BIG)
        selb_ref[h] = jnp.where(n_idx < own, chosen, jnp.where(n_idx == own, 0.0, NEG_BIG))

    def score_body(j, maxes):
        d, s0 = _tile_pos(j, jmax)
        n = j // (MOBA_BLOCK // TK)
        out = []
        for h in range(N_HEADS):
            z = _dot_nt(kx_s[pl.ds(s0, TK), _head(h)], qx_s[:, _head(h)])
            z = (z + tb_ref[h, d] + selb_ref[h, pl.ds(n, 1), :]) * LOG2E
            z_s[h, pl.ds(s0, TK), :] = z
            out.append(jnp.maximum(maxes[h], _fold8(z, jnp.max)))
        return tuple(out)

    maxes = _tile_loop(jmax + 1, score_body, _neg_rows())
    _softmax_passes(z_s, p_s, maxes, vt_ref, o_ref, acc_ref, jmax + 1)


def _fox_kernel(q_ref, k_ref, pq_ref, pk_ref, vt_ref, tb_ref, o_ref, acc_ref, z_s, p_s, qx_s, kx_s):
    i = pl.program_id(1)
    jmax = (i + 1) * Q_PER_K - 1

    for h in range(N_HEADS):
        cols = slice(h * HEAD_DIM, (h + 1) * HEAD_DIM)
        qx_s[:, h * 128:h * 128 + HEAD_DIM] = q_ref[0, :, cols]
        qx_s[:, h * 128 + HEAD_DIM:(h + 1) * 128] = pq_ref[0, :, cols]

    @pl.when(i == 0)
    def _():
        for h in range(N_HEADS):
            cols = slice(h * HEAD_DIM, (h + 1) * HEAD_DIM)
            kx_s[:, h * 128:h * 128 + HEAD_DIM] = k_ref[0, :, cols]
            kx_s[:, h * 128 + HEAD_DIM:(h + 1) * 128] = pk_ref[0, :, cols]

    def score_body(j, maxes):
        d, s0 = _tile_pos(j, jmax)
        causal = tb_ref[0, d]
        out = []
        for h in range(N_HEADS):
            ext = slice(h * 128, (h + 1) * 128)
            z = (_dot_nt(kx_s[pl.ds(s0, TK), ext], qx_s[:, ext]) + causal) * LOG2E
            z_s[h, pl.ds(s0, TK), :] = z
            out.append(jnp.maximum(maxes[h], _fold8(z, jnp.max)))
        return tuple(out)

    maxes = _tile_loop(jmax + 1, score_body, _neg_rows())
    _softmax_passes(z_s, p_s, maxes, vt_ref, o_ref, acc_ref, jmax + 1)


I16 = jnp.int16
HALF = 1 << 15


def _count16(ref, n_tiles, thr16, strict):
    def body(j, cnt):
        t = ref[j]
        hit = (t > thr16) if strict else (t >= thr16)
        g = jnp.where(hit, jnp.int16(1), jnp.int16(0)).reshape(TK // 16, 16, TQ)
        for r in range(TK // 16):
            cnt = cnt + g[r]
        return cnt

    cnt = _tile_loop(n_tiles, body, jnp.zeros((16, TQ), I16))
    return jnp.sum(cnt.astype(I32), axis=0, keepdims=True)


def _radix16(ref, n_tiles, need):
    def bit_body(it, t_u):
        cand = t_u | (jnp.int32(1) << (15 - it))
        cnt = _count16(ref, n_tiles, (cand - HALF).astype(I16), False)
        return jnp.where(cnt >= need, cand, t_u)

    return lax.fori_loop(0, 16, bit_body, jnp.zeros((1, TQ), I32))


def _dsa_kernel(q_ref, iq_ref, kv_ref, vt_ref, iw_ref, tb_ref, tc_ref, o_ref,
                acc_ref, z_s, p_s, keys_ref, khi_ref, klo_ref, qx_s, iqx_s, *, topk):
    i = pl.program_id(1)
    jmax = (i + 1) * Q_PER_K - 1
    _align_heads(qx_s, q_ref[0])
    _align_heads(iqx_s, iq_ref[0], IDX_DIM)

    def index_body(j, carry):
        d, s0 = _tile_pos(j, jmax)
        k_idx = kv_ref[0, pl.ds(s0, TK), HEAD_DIM:HEAD_DIM + IDX_DIM]
        sc = jnp.zeros((TK, TQ), F32)
        for hh in range(IDX_HEADS):
            qi = iqx_s[:, _head(hh, IDX_DIM)]
            sc = sc + iw_ref[0, hh:hh + 1, :] * jnp.maximum(_dot_nt(k_idx, qi), 0.0)
        sc = jnp.where(sc == 0.0, 0.0, sc)
        bits = lax.bitcast_convert_type(sc, I32)
        key = bits ^ ((bits >> 31) & 0x7FFFFFFF)
        key = jnp.where(tc_ref[0, d] < 0.0, INT_MIN, key)
        keys_ref[j] = key
        khi_ref[j] = (key >> 16).astype(I16)
        klo_ref[j] = ((key & 0xFFFF) - HALF).astype(I16)
        return carry

    _tile_loop(jmax + 1, index_body, 0)

    n_tiles = jmax + 1
    t_hi = _radix16(khi_ref, n_tiles, topk)
    t_hi16 = (t_hi - HALF).astype(I16)
    need_lo = topk - _count16(khi_ref, n_tiles, t_hi16, True)

    def restrict_body(j, carry):
        klo_ref[j] = jnp.where(khi_ref[j] == t_hi16, klo_ref[j], jnp.int16(-HALF))
        return carry

    _tile_loop(n_tiles, restrict_body, 0)
    t_lo = _radix16(klo_ref, n_tiles, need_lo)
    thr = (t_hi - HALF) * (1 << 16) + t_lo
    n_above = (topk - need_lo) + _count16(klo_ref, n_tiles, (t_lo - HALF).astype(I16), True)
    need_eq = (topk - n_above).astype(F32)

    row = lax.broadcasted_iota(I32, (TK + 16, TK), 0)
    col = lax.broadcasted_iota(I32, (TK + 16, TK), 1)
    lower = jnp.where((col < row) | (row >= TK), 1.0, 0.0).astype(BF16)

    def score_body(j, carry):
        maxes, seen = carry
        d, s0 = _tile_pos(j, jmax)
        k_j = kv_ref[0, pl.ds(s0, TK), 0:HEAD_DIM]
        key = keys_ref[j]
        is_thr = key == thr
        r = _dot(lower, jnp.where(is_thr, 1.0, 0.0).astype(BF16))
        tie = jnp.where(r[:TK] + seen < need_eq, 0.0, NEG_BIG)
        selb = jnp.where(key > thr, 0.0, jnp.where(is_thr, tie, NEG_BIG))
        out = []
        for h in range(N_HEADS):
            z = (_dot_nt(k_j, qx_s[:, _head(h)]) + tb_ref[h, d] + selb) * LOG2E
            z_s[h, pl.ds(s0, TK), :] = z
            out.append(jnp.maximum(maxes[h], _fold8(z, jnp.max)))
        return tuple(out), seen + r[TK:TK + 1]

    maxes, _ = _tile_loop(jmax + 1, score_body, (_neg_rows(), jnp.zeros((1, TQ), F32)))
    _softmax_passes(z_s, p_s, maxes, vt_ref, o_ref, acc_ref, jmax + 1, shared_v=True)


def _mixer_call(kernel_fn, name, b, s, in_specs, scratch, args):
    return pl.pallas_call(
        kernel_fn,
        grid=(b, s // TQ),
        in_specs=in_specs,
        out_specs=pl.BlockSpec((1, TQ, BRANCH_W), lambda bi, i: (bi, i, 0)),
        out_shape=jax.ShapeDtypeStruct((b, s, BRANCH_W), BF16),
        scratch_shapes=scratch,
        compiler_params=_params("arbitrary", "arbitrary"),
        name=name,
    )(*args)


def _q_spec(width, col):
    return pl.BlockSpec((1, TQ, width), lambda bi, i: (bi, i, col))


def _k_spec(s, width, col):
    return pl.BlockSpec((1, s, width), lambda bi, i: (bi, 0, col))


def _vt_spec(s, rows, row_blk):
    return pl.BlockSpec((1, rows, s), lambda bi, i: (bi, row_blk, 0))


def _tile_spec(n, blk):
    return pl.BlockSpec((n, N_DIAG + 1, TK, TQ), lambda bi, i: (blk, 0, 0, 0))


_ACC = lambda: pltpu.VMEM((BRANCH_W, TQ), F32)
_SCORES = lambda s: pltpu.VMEM((N_HEADS, s, TQ), F32)
_WEIGHTS = lambda s: pltpu.VMEM((N_HEADS, s, TQ), BF16)
_HEADS128 = lambda rows: pltpu.VMEM((rows, N_HEADS * 128), BF16)


def _sb(main, vt, tiles):
    b, s, _ = main.shape
    return _mixer_call(
        _sb_kernel, "sb_mixer", b, s,
        [_q_spec(BRANCH_W, 0), _k_spec(s, BRANCH_W, 1), _vt_spec(s, BRANCH_W, 0),
         _tile_spec(2, KIND_STRICT_ADD // 2)],
        [_ACC(), _SCORES(s), pltpu.VMEM((N_HEADS, s // TK, 2 * TK, TQ), BF16), _WEIGHTS(s)],
        (main, main, vt, tiles))


def _moba(main, vt, tiles):
    b, s, _ = main.shape
    return _mixer_call(
        _moba_kernel, "moba_mixer", b, s,
        [_q_spec(BRANCH_W, 2), _k_spec(s, BRANCH_W, 3), _vt_spec(s, BRANCH_W, 1),
         _tile_spec(N_HEADS, 0)],
        [_ACC(), _SCORES(s), _WEIGHTS(s), pltpu.VMEM((N_HEADS, 8, TQ), F32),
         pltpu.VMEM((8, BRANCH_W), F32), _HEADS128(TQ), _HEADS128(s)], (main, main, vt, tiles))


def _fox(main, pq, pk, vt, tiles):
    b, s, _ = main.shape
    return _mixer_call(
        _fox_kernel, "fox_mixer", b, s,
        [_q_spec(BRANCH_W, 4), _k_spec(s, BRANCH_W, 5), _q_spec(BRANCH_W, 0), _k_spec(s, BRANCH_W, 0),
         _vt_spec(s, BRANCH_W, 2), _tile_spec(1, KIND_CAUSAL)],
        [_ACC(), _SCORES(s), _WEIGHTS(s), _HEADS128(TQ), _HEADS128(s)],
        (main, main, pq, pk, vt, tiles))


def _dsa(main, kv, vt, iw, tiles):
    b, s, _ = main.shape
    topk = min(DSA_TOPK, s // 4)
    return _mixer_call(
        functools.partial(_dsa_kernel, topk=topk), "dsa_mixer", b, s,
        [_q_spec(BRANCH_W, 6), _q_spec(BRANCH_W, 7), _k_spec(s, 128, 0),
         _vt_spec(s, HEAD_DIM, 3 * BRANCH_W // HEAD_DIM),
         pl.BlockSpec((1, 16, TQ), lambda bi, i: (bi, 0, i)),
         _tile_spec(N_HEADS, 1), _tile_spec(1, KIND_CAUSAL)],
        [_ACC(), _SCORES(s), _WEIGHTS(s), pltpu.VMEM((s // TK, TK, TQ), I32),
         pltpu.VMEM((s // TK, TK, TQ), I16), pltpu.VMEM((s // TK, TK, TQ), I16), _HEADS128(TQ),
         pltpu.VMEM((TQ, IDX_HEADS * 128), BF16)],
        (main, main, kv, vt, iw, tiles, tiles))


def _combine_kernel(x_ref, g_ref, o0, o1, o2, o3, wg_ref, wb_ref, wo_ref, out_ref):
    x = x_ref[...]
    xn = _rmsnorm(x, g_ref[...]).astype(BF16)
    mixed = None
    for i, o_ref in enumerate((o0, o1, o2, o3)):
        gate = 1.0 / (1.0 + jnp.exp(-_dot(xn, wg_ref[i])))
        y = gate * _dot(o_ref[...], wb_ref[i])
        mixed = y if mixed is None else mixed + y
    out_ref[...] = x + _dot(mixed.astype(BF16), wo_ref[...])


def _combine(x2, g, outs, wg, wb, wo):
    t, d = x2.shape
    tm = 256
    row = lambda i: (i, 0)
    return pl.pallas_call(
        _combine_kernel,
        grid=(t // tm,),
        in_specs=[pl.BlockSpec((tm, d), row), pl.BlockSpec((1, d), lambda i: (0, 0))]
        + [pl.BlockSpec((tm, BRANCH_W), row)] * N_BRANCH
        + [pl.BlockSpec((N_BRANCH, d, d), lambda i: (0, 0, 0)),
           pl.BlockSpec((N_BRANCH, BRANCH_W, d), lambda i: (0, 0, 0)),
           pl.BlockSpec((d, d), lambda i: (0, 0))],
        out_specs=pl.BlockSpec((tm, d), row),
        out_shape=jax.ShapeDtypeStruct((t, d), F32),
        compiler_params=_params("arbitrary"),
        name="gate_combine_out",
    )(x2, g.reshape(1, d), *outs, wg, wb, wo)


FFN_CHUNK = 256
HALO = 16


def _gelu(g):
    return 0.5 * g * (1.0 + lax.erf(g * (2.0 ** -0.5)))


def _ffn_kernel(x_ref, xp_ref, g_ref, gf_ref, wup_ref, cw_ref, wd_ref, o_ref, xn_s, hg_s, hu_s, *, final_norm):
    i = pl.program_id(1)
    tm = x_ref.shape[1]
    x = x_ref[0]
    gain = g_ref[...]
    prev = _rmsnorm(xp_ref[0], gain) * jnp.where(i > 0, 1.0, 0.0)
    xn_s[...] = jnp.concatenate([jnp.zeros_like(prev), prev, _rmsnorm(x, gain)], axis=0).astype(BF16)

    def taps(h_s, cw, r):
        out = cw[r + 3:r + 4, :]
        for j in range(CONV_W):
            out = out + cw[r + j:r + j + 1, :] * h_s[HALO - (CONV_W - 1) + j:HALO - (CONV_W - 1) + j + tm, :]
        return out

    f = wd_ref.shape[0]
    nc = f // FFN_CHUNK
    chunk = lambda c: slice(c * FFN_CHUNK, (c + 1) * FFN_CHUNK)

    def up(c):
        xn = xn_s[...]
        hg_s[c % 2] = _dot(xn, wup_ref[:, chunk(c)])
        hu_s[c % 2] = _dot(xn, wup_ref[:, f + c * FFN_CHUNK:f + (c + 1) * FFN_CHUNK])

    up(0)
    acc = x
    for c in range(nc):
        if c + 1 < nc:
            up(c + 1)
        cw = cw_ref[:, chunk(c)]
        a = _gelu(taps(hg_s.at[c % 2], cw, 0)) * taps(hu_s.at[c % 2], cw, 4)
        acc = acc + _dot(a.astype(BF16), wd_ref[chunk(c), :])
    o_ref[0] = _rmsnorm(acc, gf_ref[...]) if final_norm else acc


def _ffn(x, g, g_final, final_norm, w_up, cw, w_down):
    b, s, d = x.shape
    tm = 256
    f = w_down.shape[0]
    whole = lambda bi, i: (0, 0)
    return pl.pallas_call(
        functools.partial(_ffn_kernel, final_norm=final_norm),
        grid=(b, s // tm),
        in_specs=[
            pl.BlockSpec((1, tm, d), lambda bi, i: (bi, i, 0)),
            pl.BlockSpec((1, 8, d), lambda bi, i: (bi, jnp.maximum(i * (tm // 8) - 1, 0), 0)),
            pl.BlockSpec((1, d), lambda bi, i: (0, 0)),
            pl.BlockSpec((1, d), lambda bi, i: (0, 0)),
            pl.BlockSpec((d, 2 * f), whole),
            pl.BlockSpec((8, f), whole),
            pl.BlockSpec((f, d), whole),
        ],
        out_specs=pl.BlockSpec((1, tm, d), lambda bi, i: (bi, i, 0)),
        out_shape=jax.ShapeDtypeStruct((b, s, d), F32),
        scratch_shapes=[pltpu.VMEM((tm + HALO, d), BF16),
                        pltpu.VMEM((2, tm + HALO, FFN_CHUNK), F32),
                        pltpu.VMEM((2, tm + HALO, FFN_CHUNK), F32)],
        compiler_params=_params("arbitrary", "arbitrary"),
        name="conv_ffn",
    )(x, x, g.reshape(1, d), g_final.reshape(1, d), w_up, cw, w_down)


def _pack_ffn_weights(w_up, conv_w, conv_b, w_down):
    f = w_down.shape[0]
    assert f % FFN_CHUNK == 0
    cw = jnp.concatenate([conv_w[:, :f], conv_b[None, :f], conv_w[:, f:], conv_b[None, f:]], axis=0)
    return w_up.astype(BF16), cw, w_down.astype(BF16)


def _token_mixers(x, g, w_in, fox_b, tiles):
    w_nat, w_t = _pack_in_weights(w_in)
    main, kv, fxf, vt, iw = _proj(x, g, w_nat, w_t)
    pk, pq = _fox_c(fxf, fox_b)
    return (_sb(main, vt, tiles), _moba(main, vt, tiles), _fox(main, pq, pk, vt, tiles),
            _dsa(main, kv, vt, iw, tiles))


def kernel(x, norm_mix_g, norm_ffn_g, norm_final_g, w_in, fox_b_f, t5_bias, w_gate, w_branch, w_out,
           w_up, conv_w, conv_b, w_down):
    b, s, d = x.shape
    assert s % TQ == 0 and TQ == MOBA_BLOCK and s // MOBA_BLOCK <= 8
    tiles = _make_tiles(t5_bias)
    depth = w_in.shape[0]
    for l in range(depth):
        outs = _token_mixers(x, norm_mix_g[l], w_in[l], fox_b_f[l], tiles)
        outs = [o.reshape(b * s, BRANCH_W) for o in outs]
        x = _combine(x.reshape(b * s, d), norm_mix_g[l], outs, w_gate[l].astype(BF16),
                     w_branch[l].astype(BF16), w_out[l].astype(BF16)).reshape(b, s, d)
        x = _ffn(x, norm_ffn_g[l], norm_final_g, l == depth - 1,
                 *_pack_ffn_weights(w_up[l], conv_w[l], conv_b[l], w_down[l]))
    return x
```

```python
import functools
import math

import jax
import jax.numpy as jnp
from jax import lax
from jax.experimental import pallas as pl
from jax.experimental.pallas import tpu as pltpu

F32 = jnp.float32
BF16 = jnp.bfloat16
I32 = jnp.int32

HEAD_DIM = 64
N_HEADS = 4
BRANCH_W = N_HEADS * HEAD_DIM
N_BRANCH = 4
MOBA_BLOCK = 256
MOBA_TOPK = 3
DSA_TOPK = 256
IDX_HEADS = 8
IDX_DIM = 32
T5_BUCKETS = 32
T5_MAX_DIST = 128
CONV_W = 3
RMS_EPS = 1e-6
NEG_BIG = -1e30
INT_MIN = -2147483648
LOG2E = math.log2(math.e)

TQ = 256
TK = 128
Q_PER_K = TQ // TK
T5_CONST_DIST = 113
N_DIAG = (TQ + T5_CONST_DIST - 2) // TK + 1
FAR_DIST = 1 << 20

KIND_CAUSAL = 8
KIND_STRICT_ADD = 10
KIND_STRICT_MUL = 11
N_KINDS = 12

VMEM_LIMIT = 56 * 1024 * 1024


def _dot(a, b):
    return jnp.dot(a, b, preferred_element_type=F32)


def _dot_nt(a, b):
    return lax.dot_general(a, b, (((1,), (1,)), ((), ())), preferred_element_type=F32)


def _rmsnorm(x, g):
    return x * lax.rsqrt(jnp.mean(x * x, axis=-1, keepdims=True) + RMS_EPS) * g


def _softplus(z):
    return jnp.maximum(z, 0.0) + jnp.log1p(jnp.exp(-jnp.abs(z)))


def _params(*sem):
    return pltpu.CompilerParams(dimension_semantics=sem, vmem_limit_bytes=VMEM_LIMIT)


def _tiles_kernel(tab_ref, out_ref):
    kind = pl.program_id(0)
    d = pl.program_id(1)
    sl = lax.broadcasted_iota(I32, (TK, TQ), 0)
    tl = lax.broadcasted_iota(I32, (TK, TQ), 1)
    dist = jnp.where(d == N_DIAG, FAR_DIST, (TK - TQ) + d * TK + tl - sl)
    n = jnp.maximum(dist, 0)
    max_exact = T5_BUCKETS // 2
    nf = jnp.maximum(n, 1).astype(F32)
    large = max_exact + (jnp.log(nf / max_exact) / math.log(T5_MAX_DIST / max_exact)
                         * (T5_BUCKETS - max_exact)).astype(I32)
    large = jnp.minimum(large, T5_BUCKETS - 1)
    bucket = jnp.where(n < max_exact, n, large)
    h = jnp.minimum(kind, 2 * N_HEADS - 1)
    bias = jnp.zeros((TK, TQ), F32)
    for b in range(T5_BUCKETS):
        bias = jnp.where(bucket == b, tab_ref[b, h], bias)
    causal = dist >= 0
    strict = dist > 0
    v_bias = jnp.where(causal, bias, NEG_BIG)
    v_causal = jnp.where(causal, 0.0, NEG_BIG)
    v_sadd = jnp.where(strict, 0.0, NEG_BIG)
    v_smul = jnp.where(strict, 1.0, 0.0)
    out_ref[0, 0] = jnp.where(kind < KIND_CAUSAL, v_bias,
                              jnp.where(kind < KIND_STRICT_ADD, v_causal,
                                        jnp.where(kind == KIND_STRICT_ADD, v_sadd, v_smul)))


def _make_tiles(t5_bias):
    return pl.pallas_call(
        _tiles_kernel,
        grid=(N_KINDS, N_DIAG + 1),
        in_specs=[pl.BlockSpec(memory_space=pltpu.SMEM)],
        out_specs=pl.BlockSpec((1, 1, TK, TQ), lambda k, d: (k, d, 0, 0)),
        out_shape=jax.ShapeDtypeStruct((N_KINDS, N_DIAG + 1, TK, TQ), F32),
        compiler_params=_params("arbitrary", "arbitrary"),
        name="t5_tiles",
    )(t5_bias)


N_MAIN = 8 * BRANCH_W
FOX_EXTRA = 6
N_NAT = N_MAIN + 128 + BRANCH_W
N_VT = 3 * BRANCH_W + HEAD_DIM
N_T = N_VT + 16
PROJ_CHUNK = 512


def _proj_kernel(x_ref, g_ref, w_ref, wt_ref, main_ref, kv_ref, fxf_ref, vt_ref, iw_ref):
    xn = _rmsnorm(x_ref[0], g_ref[...]).astype(BF16)
    for c in range(0, N_MAIN, PROJ_CHUNK):
        main_ref[0, :, c:c + PROJ_CHUNK] = _dot(xn, w_ref[:, c:c + PROJ_CHUNK]).astype(BF16)
    r = _dot(xn, w_ref[:, N_MAIN:N_NAT])
    kv_ref[0] = r[:, :128].astype(BF16)
    fxf_ref[0] = r[:, 128:]
    t = _dot_nt(wt_ref[...], xn)
    vt_ref[0] = t[:N_VT].astype(BF16)
    iw_ref[0] = t[N_VT:]


def _proj(x, g, w_nat, w_t):
    b, s, d = x.shape
    tm = min(512, s)
    return pl.pallas_call(
        _proj_kernel,
        grid=(b, s // tm),
        in_specs=[
            pl.BlockSpec((1, tm, d), lambda bi, i: (bi, i, 0)),
            pl.BlockSpec((1, d), lambda bi, i: (0, 0)),
            pl.BlockSpec((d, N_NAT), lambda bi, i: (0, 0)),
            pl.BlockSpec((N_T, d), lambda bi, i: (0, 0)),
        ],
        out_specs=[
            pl.BlockSpec((1, tm, N_MAIN), lambda bi, i: (bi, i, 0)),
            pl.BlockSpec((1, tm, 128), lambda bi, i: (bi, i, 0)),
            pl.BlockSpec((1, tm, BRANCH_W), lambda bi, i: (bi, i, 0)),
            pl.BlockSpec((1, N_VT, tm), lambda bi, i: (bi, 0, i)),
            pl.BlockSpec((1, 16, tm), lambda bi, i: (bi, 0, i)),
        ],
        out_shape=[
            jax.ShapeDtypeStruct((b, s, N_MAIN), BF16),
            jax.ShapeDtypeStruct((b, s, 128), BF16),
            jax.ShapeDtypeStruct((b, s, BRANCH_W), F32),
            jax.ShapeDtypeStruct((b, N_VT, s), BF16),
            jax.ShapeDtypeStruct((b, 16, s), F32),
        ],
        compiler_params=_params("arbitrary", "arbitrary"),
        name="norm_in_proj",
    )(x, g.reshape(1, d), w_nat, w_t)


def _pack_in_weights(w_in):
    d = w_in.shape[0]
    sizes = ([BRANCH_W] * 9 + [N_HEADS] + [BRANCH_W, HEAD_DIM, HEAD_DIM]
             + [IDX_HEADS * IDX_DIM, IDX_DIM, IDX_HEADS])
    offs = [0]
    for sz in sizes:
        offs.append(offs[-1] + sz)
    seg = [w_in[:, offs[i]:offs[i + 1]] for i in range(len(sizes))]
    (sb_q, sb_k, sb_v, mb_q, mb_k, mb_v, fx_q, fx_k, fx_v, fx_f,
     ds_q, ds_k, ds_v, ix_q, ix_k, ix_w) = seg
    scale = HEAD_DIM ** -0.5
    z = lambda n: jnp.zeros((d, n), w_in.dtype)
    w_nat = jnp.concatenate(
        [sb_q * scale, sb_k, mb_q * scale, mb_k, fx_q * scale, fx_k, ds_q * scale, ix_q,
         ds_k, ix_k, z(128 - HEAD_DIM - IDX_DIM)]
        + [w for h in range(N_HEADS) for w in [fx_f[:, h:h + 1]] * FOX_EXTRA + [z(HEAD_DIM - FOX_EXTRA)]],
        axis=1).astype(BF16)
    w_t = jnp.concatenate([sb_v, mb_v, fx_v, ds_v, ix_w, z(16 - IDX_HEADS)], axis=1).T.astype(BF16)
    return w_nat, w_t


CUM_BLOCK = 256


def _fox_c_kernel(f_ref, b_ref, pk_ref, pq_ref):
    s = f_ref.shape[1]
    row = lax.broadcasted_iota(I32, (CUM_BLOCK, CUM_BLOCK), 0)
    col = lax.broadcasted_iota(I32, (CUM_BLOCK, CUM_BLOCK), 1)
    tri = jnp.where(row >= col, 1.0, 0.0).astype(BF16)
    tri3 = jnp.concatenate([tri, tri, tri], axis=1)
    lane = lax.broadcasted_iota(I32, (CUM_BLOCK, BRANCH_W), 1) & (HEAD_DIM - 1)
    carry = jnp.zeros((1, BRANCH_W), F32)
    for blk in range(s // CUM_BLOCK):
        rows = slice(blk * CUM_BLOCK, (blk + 1) * CUM_BLOCK)
        f = f_ref[0, rows, :] + b_ref[...]
        ls = -_softplus(-f)
        hi = ls.astype(BF16)
        r1 = ls - hi.astype(F32)
        mid = r1.astype(BF16)
        lo = (r1 - mid.astype(F32)).astype(BF16)
        c = _dot(tri3, jnp.concatenate([hi, mid, lo], axis=0)) + carry
        carry = c[CUM_BLOCK - 1:CUM_BLOCK, :]
        c_hi = c.astype(BF16).astype(F32)
        r2 = c - c_hi
        c_mid = r2.astype(BF16).astype(F32)
        c_lo = r2 - c_mid
        third = jnp.where(lane >= 3, lane - 3, lane)
        piece = jnp.where(third == 0, c_hi, jnp.where(third == 1, c_mid, c_lo))
        pk = jnp.where(lane < 3, piece, jnp.where(lane < FOX_EXTRA, 1.0, 0.0))
        pq = jnp.where(lane < 3, -1.0, jnp.where(lane < FOX_EXTRA, piece, 0.0))
        pk_ref[0, rows, :] = pk.astype(BF16)
        pq_ref[0, rows, :] = pq.astype(BF16)


def _fox_c(fxf, fox_b):
    b, s, _ = fxf.shape
    bias = jnp.concatenate([v for h in range(N_HEADS)
                            for v in [fox_b[h:h + 1]] * FOX_EXTRA + [jnp.zeros((HEAD_DIM - FOX_EXTRA,), F32)]])
    spec = pl.BlockSpec((1, s, BRANCH_W), lambda bi: (bi, 0, 0))
    return pl.pallas_call(
        _fox_c_kernel,
        grid=(b,),
        in_specs=[spec, pl.BlockSpec((1, BRANCH_W), lambda bi: (0, 0))],
        out_specs=[spec, spec],
        out_shape=[jax.ShapeDtypeStruct((b, s, BRANCH_W), BF16)] * 2,
        compiler_params=_params("arbitrary"),
        name="fox_cumgate",
    )(fxf, bias.reshape(1, BRANCH_W))


def _tile_pos(j, jmax):
    return jnp.minimum(jmax - j, N_DIAG), pl.multiple_of(j * TK, TK)


def _tile_loop(n_tiles, body, init, first=0):
    def trip(t, carry):
        for u in range(Q_PER_K):
            carry = body(t * Q_PER_K + u, carry)
        return carry

    return lax.fori_loop(first // Q_PER_K, n_tiles // Q_PER_K, trip, init)


def _fold8(x, op):
    return op(x.reshape(TK // 8, 8, TQ), axis=0)


def _align_heads(dst_s, src, width=HEAD_DIM):
    for h in range(src.shape[1] // width):
        dst_s[:, h * 128:h * 128 + width] = src[:, h * width:(h + 1) * width]


def _head(h, width=HEAD_DIM):
    return slice(h * 128, h * 128 + width)


def _neg_rows():
    return tuple(jnp.full((8, TQ), -1e38, F32) for _ in range(N_HEADS))


def _weights_matmul(w_s, vt_ref, o_ref, acc_ref, n_tiles, inv_l, shared_v):
    s = w_s.shape[1]
    half = s // 2
    use_second = n_tiles * TK > half

    def zero_body(j, carry):
        s0 = pl.multiple_of(j * TK, TK)
        for h in range(N_HEADS):
            w_s[h, pl.ds(s0, TK), :] = jnp.zeros((TK, TQ), BF16)
        return carry

    lax.fori_loop(n_tiles, jnp.where(use_second, s // TK, half // TK), zero_body, 0)

    def product(lo, first):
        for h in range(N_HEADS):
            rows = slice(h * HEAD_DIM, (h + 1) * HEAD_DIM)
            vt = vt_ref[0, :, lo:lo + half] if shared_v else vt_ref[0, rows, lo:lo + half]
            out = _dot(vt, w_s[h, lo:lo + half, :])
            acc_ref[rows, :] = out if first else acc_ref[rows, :] + out

    product(0, True)
    pl.when(use_second)(functools.partial(product, half, False))
    if inv_l is not None:
        for h in range(N_HEADS):
            rows = slice(h * HEAD_DIM, (h + 1) * HEAD_DIM)
            acc_ref[rows, :] = acc_ref[rows, :] * inv_l[h]
    o_ref[0] = acc_ref[...].T.astype(o_ref.dtype)


def _softmax_passes(z_s, p_s, maxes, vt_ref, o_ref, acc_ref, n_tiles, shared_v=False):
    m = [jnp.max(mx, axis=0, keepdims=True) for mx in maxes]

    def exp_body(j, sums):
        s0 = pl.multiple_of(j * TK, TK)
        out = []
        for h in range(N_HEADS):
            p = jnp.exp2(z_s[h, pl.ds(s0, TK), :] - m[h])
            p_s[h, pl.ds(s0, TK), :] = p.astype(BF16)
            out.append(sums[h] + _fold8(p, jnp.sum))
        return tuple(out)

    sums = lax.fori_loop(0, n_tiles, exp_body, tuple(jnp.zeros((8, TQ), F32) for _ in range(N_HEADS)))
    inv_l = [1.0 / jnp.sum(sm, axis=0, keepdims=True) for sm in sums]
    _weights_matmul(p_s, vt_ref, o_ref, acc_ref, n_tiles, inv_l, shared_v)


def _sb_kernel(q_ref, k_ref, vt_ref, tb_ref, o_ref, acc_ref, base_s, hl_s, w_s):
    i = pl.program_id(1)
    jmax = (i + 1) * Q_PER_K - 1

    def score_body(masked, j, carry):
        d, s0 = _tile_pos(j, jmax)
        for h in range(N_HEADS):
            cols = slice(h * HEAD_DIM, (h + 1) * HEAD_DIM)
            z = _dot_nt(k_ref[0, pl.ds(s0, TK), cols], q_ref[0, :, cols])
            sp = jnp.maximum(z, 0.0) + jnp.log(1.0 + jnp.exp(-jnp.abs(z)))
            logsig = z - sp
            base_s[h, pl.ds(s0, TK), :] = logsig + tb_ref[0, d] if masked else logsig
            pos = sp * tb_ref[1, d] if masked else sp
            hi = pos.astype(BF16)
            lo = (pos - hi.astype(F32)).astype(BF16)
            hl_s[h, j] = jnp.concatenate([hi, lo], axis=0)
        return carry

    n_tiles = jmax + 1
    n_plain = jnp.maximum(n_tiles - Q_PER_K * ((N_DIAG + Q_PER_K - 1) // Q_PER_K), 0)
    _tile_loop(n_plain, functools.partial(score_body, False), 0)
    _tile_loop(n_tiles, functools.partial(score_body, True), 0, first=n_plain)

    row = lax.broadcasted_iota(I32, (TK + 16, 2 * TK), 0)
    col = lax.broadcasted_iota(I32, (TK + 16, 2 * TK), 1)
    upper = jnp.where(((col & (TK - 1)) > row) | (row >= TK), -1.0, 0.0).astype(BF16)

    def weight_body(jj, suffix):
        j = jmax - jj
        s0 = pl.multiple_of(j * TK, TK)
        out = []
        for h in range(N_HEADS):
            r = _dot(upper, hl_s[h, j])
            w = jnp.exp(base_s[h, pl.ds(s0, TK), :] + r[:TK] + suffix[h])
            w_s[h, pl.ds(s0, TK), :] = w.astype(BF16)
            out.append(suffix[h] + r[TK:TK + 1])
        return tuple(out)

    _tile_loop(jmax + 1, weight_body, tuple(jnp.zeros((1, TQ), F32) for _ in range(N_HEADS)))
    _weights_matmul(w_s, vt_ref, o_ref, acc_ref, jmax + 1, None, False)


def _moba_kernel(q_ref, k_ref, vt_ref, tb_ref, o_ref, acc_ref, z_s, p_s, selb_ref, km_s, qx_s, kx_s):
    i = pl.program_id(1)
    own = i
    jmax = (i + 1) * Q_PER_K - 1
    nb = k_ref.shape[1] // MOBA_BLOCK
    _align_heads(qx_s, q_ref[0])
    pl.when(i == 0)(lambda: _align_heads(kx_s, k_ref[0]))

    @pl.when(i == 0)
    def _():
        s_len = k_ref.shape[1]
        blk = lax.broadcasted_iota(I32, (8, s_len), 1) >> (MOBA_BLOCK.bit_length() - 1)
        avg = jnp.where(blk == lax.broadcasted_iota(I32, (8, s_len), 0), 1.0 / MOBA_BLOCK, 0.0)
        km_s[...] = _dot(avg.astype(BF16), k_ref[0])

    km_all = km_s[...]
    n_idx = lax.broadcasted_iota(I32, (8, TQ), 0)
    for h in range(N_HEADS):
        cols = slice(h * HEAD_DIM, (h + 1) * HEAD_DIM)
        q_h = q_ref[0, :, cols]
        km = km_all[:, cols]
        km_hi = km.astype(BF16)
        km_lo = (km - km_hi.astype(F32)).astype(BF16)
        gate = _dot_nt(km_hi, q_h) + _dot_nt(km_lo, q_h)
        rank = jnp.zeros((8, TQ), F32)
        for m in range(nb):
            g_m = gate[m:m + 1, :]
            wins = jnp.where(n_idx > m, jnp.where(g_m >= gate, 1.0, 0.0), jnp.where(g_m > gate, 1.0, 0.0))
            rank = rank + wins * jnp.where(m < own, 1.0, 0.0)
        chosen = jnp.where(rank < MOBA_TOPK, 0.0, NEG_BIG)
        selb_ref[h] = jnp.where(n_idx < own, chosen, jnp.where(n_idx == own, 0.0, NEG_BIG))

    def score_body(j, maxes):
        d, s0 = _tile_pos(j, jmax)
        n = j // (MOBA_BLOCK // TK)
        out = []
        for h in range(N_HEADS):
            z = _dot_nt(kx_s[pl.ds(s0, TK), _head(h)], qx_s[:, _head(h)])
            z = (z + tb_ref[h, d] + selb_ref[h, pl.ds(n, 1), :]) * LOG2E
            z_s[h, pl.ds(s0, TK), :] = z
            out.append(jnp.maximum(maxes[h], _fold8(z, jnp.max)))
        return tuple(out)

    maxes = _tile_loop(jmax + 1, score_body, _neg_rows())
    _softmax_passes(z_s, p_s, maxes, vt_ref, o_ref, acc_ref, jmax + 1)


def _fox_kernel(q_ref, k_ref, pq_ref, pk_ref, vt_ref, tb_ref, o_ref, acc_ref, z_s, p_s, qx_s, kx_s):
    i = pl.program_id(1)
    jmax = (i + 1) * Q_PER_K - 1

    for h in range(N_HEADS):
        cols = slice(h * HEAD_DIM, (h + 1) * HEAD_DIM)
        qx_s[:, h * 128:h * 128 + HEAD_DIM] = q_ref[0, :, cols]
        qx_s[:, h * 128 + HEAD_DIM:(h + 1) * 128] = pq_ref[0, :, cols]

    @pl.when(i == 0)
    def _():
        for h in range(N_HEADS):
            cols = slice(h * HEAD_DIM, (h + 1) * HEAD_DIM)
            kx_s[:, h * 128:h * 128 + HEAD_DIM] = k_ref[0, :, cols]
            kx_s[:, h * 128 + HEAD_DIM:(h + 1) * 128] = pk_ref[0, :, cols]

    def score_body(j, maxes):
        d, s0 = _tile_pos(j, jmax)
        causal = tb_ref[0, d]
        out = []
        for h in range(N_HEADS):
            ext = slice(h * 128, (h + 1) * 128)
            z = (_dot_nt(kx_s[pl.ds(s0, TK), ext], qx_s[:, ext]) + causal) * LOG2E
            z_s[h, pl.ds(s0, TK), :] = z
            out.append(jnp.maximum(maxes[h], _fold8(z, jnp.max)))
        return tuple(out)

    maxes = _tile_loop(jmax + 1, score_body, _neg_rows())
    _softmax_passes(z_s, p_s, maxes, vt_ref, o_ref, acc_ref, jmax + 1)


I16 = jnp.int16
HALF = 1 << 15


def _count16(ref, n_tiles, thr16, strict):
    def body(j, cnt):
        t = ref[j]
        hit = (t > thr16) if strict else (t >= thr16)
        g = jnp.where(hit, jnp.int16(1), jnp.int16(0)).reshape(TK // 16, 16, TQ)
        for r in range(TK // 16):
            cnt = cnt + g[r]
        return cnt

    cnt = _tile_loop(n_tiles, body, jnp.zeros((16, TQ), I16))
    return jnp.sum(cnt.astype(I32), axis=0, keepdims=True)


def _radix16(ref, n_tiles, need):
    def bit_body(it, t_u):
        cand = t_u | (jnp.int32(1) << (15 - it))
        cnt = _count16(ref, n_tiles, (cand - HALF).astype(I16), False)
        return jnp.where(cnt >= need, cand, t_u)

    return lax.fori_loop(0, 16, bit_body, jnp.zeros((1, TQ), I32))


def _dsa_kernel(q_ref, iq_ref, kv_ref, vt_ref, iw_ref, tb_ref, tc_ref, o_ref,
                acc_ref, z_s, p_s, keys_ref, khi_ref, klo_ref, qx_s, iqx_s, *, topk):
    i = pl.program_id(1)
    jmax = (i + 1) * Q_PER_K - 1
    _align_heads(qx_s, q_ref[0])
    _align_heads(iqx_s, iq_ref[0], IDX_DIM)

    def index_body(j, carry):
        d, s0 = _tile_pos(j, jmax)
        k_idx = kv_ref[0, pl.ds(s0, TK), HEAD_DIM:HEAD_DIM + IDX_DIM]
        sc = jnp.zeros((TK, TQ), F32)
        for hh in range(IDX_HEADS):
            qi = iqx_s[:, _head(hh, IDX_DIM)]
            sc = sc + iw_ref[0, hh:hh + 1, :] * jnp.maximum(_dot_nt(k_idx, qi), 0.0)
        sc = jnp.where(sc == 0.0, 0.0, sc)
        bits = lax.bitcast_convert_type(sc, I32)
        key = bits ^ ((bits >> 31) & 0x7FFFFFFF)
        key = jnp.where(tc_ref[0, d] < 0.0, INT_MIN, key)
        keys_ref[j] = key
        khi_ref[j] = (key >> 16).astype(I16)
        klo_ref[j] = ((key & 0xFFFF) - HALF).astype(I16)
        return carry

    _tile_loop(jmax + 1, index_body, 0)

    n_tiles = jmax + 1
    t_hi = _radix16(khi_ref, n_tiles, topk)
    t_hi16 = (t_hi - HALF).astype(I16)
    need_lo = topk - _count16(khi_ref, n_tiles, t_hi16, True)

    def restrict_body(j, carry):
        klo_ref[j] = jnp.where(khi_ref[j] == t_hi16, klo_ref[j], jnp.int16(-HALF))
        return carry

    _tile_loop(n_tiles, restrict_body, 0)
    t_lo = _radix16(klo_ref, n_tiles, need_lo)
    thr = (t_hi - HALF) * (1 << 16) + t_lo
    n_above = (topk - need_lo) + _count16(klo_ref, n_tiles, (t_lo - HALF).astype(I16), True)
    need_eq = (topk - n_above).astype(F32)

    row = lax.broadcasted_iota(I32, (TK + 16, TK), 0)
    col = lax.broadcasted_iota(I32, (TK + 16, TK), 1)
    lower = jnp.where((col < row) | (row >= TK), 1.0, 0.0).astype(BF16)

    def score_body(j, carry):
        maxes, seen = carry
        d, s0 = _tile_pos(j, jmax)
        k_j = kv_ref[0, pl.ds(s0, TK), 0:HEAD_DIM]
        key = keys_ref[j]
        is_thr = key == thr
        r = _dot(lower, jnp.where(is_thr, 1.0, 0.0).astype(BF16))
        tie = jnp.where(r[:TK] + seen < need_eq, 0.0, NEG_BIG)
        selb = jnp.where(key > thr, 0.0, jnp.where(is_thr, tie, NEG_BIG))
        out = []
        for h in range(N_HEADS):
            z = (_dot_nt(k_j, qx_s[:, _head(h)]) + tb_ref[h, d] + selb) * LOG2E
            z_s[h, pl.ds(s0, TK), :] = z
            out.append(jnp.maximum(maxes[h], _fold8(z, jnp.max)))
        return tuple(out), seen + r[TK:TK + 1]

    maxes, _ = _tile_loop(jmax + 1, score_body, (_neg_rows(), jnp.zeros((1, TQ), F32)))
    _softmax_passes(z_s, p_s, maxes, vt_ref, o_ref, acc_ref, jmax + 1, shared_v=True)


def _mixer_call(kernel_fn, name, b, s, in_specs, scratch, args):
    return pl.pallas_call(
        kernel_fn,
        grid=(b, s // TQ),
        in_specs=in_specs,
        out_specs=pl.BlockSpec((1, TQ, BRANCH_W), lambda bi, i: (bi, i, 0)),
        out_shape=jax.ShapeDtypeStruct((b, s, BRANCH_W), BF16),
        scratch_shapes=scratch,
        compiler_params=_params("arbitrary", "arbitrary"),
        name=name,
    )(*args)


def _q_spec(width, col):
    return pl.BlockSpec((1, TQ, width), lambda bi, i: (bi, i, col))


def _k_spec(s, width, col):
    return pl.BlockSpec((1, s, width), lambda bi, i: (bi, 0, col))


def _vt_spec(s, rows, row_blk):
    return pl.BlockSpec((1, rows, s), lambda bi, i: (bi, row_blk, 0))


def _tile_spec(n, blk):
    return pl.BlockSpec((n, N_DIAG + 1, TK, TQ), lambda bi, i: (blk, 0, 0, 0))


_ACC = lambda: pltpu.VMEM((BRANCH_W, TQ), F32)
_SCORES = lambda s: pltpu.VMEM((N_HEADS, s, TQ), F32)
_WEIGHTS = lambda s: pltpu.VMEM((N_HEADS, s, TQ), BF16)
_HEADS128 = lambda rows: pltpu.VMEM((rows, N_HEADS * 128), BF16)


def _sb(main, vt, tiles):
    b, s, _ = main.shape
    return _mixer_call(
        _sb_kernel, "sb_mixer", b, s,
        [_q_spec(BRANCH_W, 0), _k_spec(s, BRANCH_W, 1), _vt_spec(s, BRANCH_W, 0),
         _tile_spec(2, KIND_STRICT_ADD // 2)],
        [_ACC(), _SCORES(s), pltpu.VMEM((N_HEADS, s // TK, 2 * TK, TQ), BF16), _WEIGHTS(s)],
        (main, main, vt, tiles))


def _moba(main, vt, tiles):
    b, s, _ = main.shape
    return _mixer_call(
        _moba_kernel, "moba_mixer", b, s,
        [_q_spec(BRANCH_W, 2), _k_spec(s, BRANCH_W, 3), _vt_spec(s, BRANCH_W, 1),
         _tile_spec(N_HEADS, 0)],
        [_ACC(), _SCORES(s), _WEIGHTS(s), pltpu.VMEM((N_HEADS, 8, TQ), F32),
         pltpu.VMEM((8, BRANCH_W), F32), _HEADS128(TQ), _HEADS128(s)], (main, main, vt, tiles))


def _fox(main, pq, pk, vt, tiles):
    b, s, _ = main.shape
    return _mixer_call(
        _fox_kernel, "fox_mixer", b, s,
        [_q_spec(BRANCH_W, 4), _k_spec(s, BRANCH_W, 5), _q_spec(BRANCH_W, 0), _k_spec(s, BRANCH_W, 0),
         _vt_spec(s, BRANCH_W, 2), _tile_spec(1, KIND_CAUSAL)],
        [_ACC(), _SCORES(s), _WEIGHTS(s), _HEADS128(TQ), _HEADS128(s)],
        (main, main, pq, pk, vt, tiles))


def _dsa(main, kv, vt, iw, tiles):
    b, s, _ = main.shape
    topk = min(DSA_TOPK, s // 4)
    return _mixer_call(
        functools.partial(_dsa_kernel, topk=topk), "dsa_mixer", b, s,
        [_q_spec(BRANCH_W, 6), _q_spec(BRANCH_W, 7), _k_spec(s, 128, 0),
         _vt_spec(s, HEAD_DIM, 3 * BRANCH_W // HEAD_DIM),
         pl.BlockSpec((1, 16, TQ), lambda bi, i: (bi, 0, i)),
         _tile_spec(N_HEADS, 1), _tile_spec(1, KIND_CAUSAL)],
        [_ACC(), _SCORES(s), _WEIGHTS(s), pltpu.VMEM((s // TK, TK, TQ), I32),
         pltpu.VMEM((s // TK, TK, TQ), I16), pltpu.VMEM((s // TK, TK, TQ), I16), _HEADS128(TQ),
         pltpu.VMEM((TQ, IDX_HEADS * 128), BF16)],
        (main, main, kv, vt, iw, tiles, tiles))


def _combine_kernel(x_ref, g_ref, o0, o1, o2, o3, wg_ref, wb_ref, wo_ref, out_ref):
    x = x_ref[...]
    xn = _rmsnorm(x, g_ref[...]).astype(BF16)
    mixed = None
    for i, o_ref in enumerate((o0, o1, o2, o3)):
        gate = 1.0 / (1.0 + jnp.exp(-_dot(xn, wg_ref[i])))
        y = gate * _dot(o_ref[...], wb_ref[i])
        mixed = y if mixed is None else mixed + y
    out_ref[...] = x + _dot(mixed.astype(BF16), wo_ref[...])


def _combine(x2, g, outs, wg, wb, wo):
    t, d = x2.shape
    tm = 256
    row = lambda i: (i, 0)
    return pl.pallas_call(
        _combine_kernel,
        grid=(t // tm,),
        in_specs=[pl.BlockSpec((tm, d), row), pl.BlockSpec((1, d), lambda i: (0, 0))]
        + [pl.BlockSpec((tm, BRANCH_W), row)] * N_BRANCH
        + [pl.BlockSpec((N_BRANCH, d, d), lambda i: (0, 0, 0)),
           pl.BlockSpec((N_BRANCH, BRANCH_W, d), lambda i: (0, 0, 0)),
           pl.BlockSpec((d, d), lambda i: (0, 0))],
        out_specs=pl.BlockSpec((tm, d), row),
        out_shape=jax.ShapeDtypeStruct((t, d), F32),
        compiler_params=_params("arbitrary"),
        name="gate_combine_out",
    )(x2, g.reshape(1, d), *outs, wg, wb, wo)


FFN_CHUNK = 256
FFN_SLOTS = 3
HALO = 16


def _gelu(g):
    return 0.5 * g * (1.0 + lax.erf(g * (2.0 ** -0.5)))


def _ffn_kernel(x_ref, xp_ref, g_ref, gf_ref, wup_ref, cw_ref, wd_ref, o_ref, xn_s, hg_s, hu_s, *, final_norm):
    i = pl.program_id(1)
    tm = x_ref.shape[1]
    x = x_ref[0]
    gain = g_ref[...]
    prev = _rmsnorm(xp_ref[0], gain) * jnp.where(i > 0, 1.0, 0.0)
    xn_s[...] = jnp.concatenate([jnp.zeros_like(prev), prev, _rmsnorm(x, gain)], axis=0).astype(BF16)

    def taps(h_s, cw, r):
        out = cw[r + 3:r + 4, :]
        for j in range(CONV_W):
            out = out + cw[r + j:r + j + 1, :] * h_s[HALO - (CONV_W - 1) + j:HALO - (CONV_W - 1) + j + tm, :]
        return out

    f = wd_ref.shape[0]
    nc = f // FFN_CHUNK
    chunk = lambda c: slice(c * FFN_CHUNK, (c + 1) * FFN_CHUNK)

    def up(c):
        xn = xn_s[...]
        hg_s[c % FFN_SLOTS] = _dot(xn, wup_ref[:, chunk(c)])
        hu_s[c % FFN_SLOTS] = _dot(xn, wup_ref[:, f + c * FFN_CHUNK:f + (c + 1) * FFN_CHUNK])

    for c in range(FFN_SLOTS - 1):
        up(c)
    acc = x
    prev = None
    for c in range(nc):
        if c + FFN_SLOTS - 1 < nc:
            up(c + FFN_SLOTS - 1)
        if prev is not None:
            acc = acc + _dot(prev, wd_ref[chunk(c - 1), :])
        cw = cw_ref[:, chunk(c)]
        slot = c % FFN_SLOTS
        prev = (_gelu(taps(hg_s.at[slot], cw, 0)) * taps(hu_s.at[slot], cw, 4)).astype(BF16)
    acc = acc + _dot(prev, wd_ref[chunk(nc - 1), :])
    o_ref[0] = _rmsnorm(acc, gf_ref[...]) if final_norm else acc


def _ffn(x, g, g_final, final_norm, w_up, cw, w_down):
    b, s, d = x.shape
    tm = 256
    f = w_down.shape[0]
    whole = lambda bi, i: (0, 0)
    return pl.pallas_call(
        functools.partial(_ffn_kernel, final_norm=final_norm),
        grid=(b, s // tm),
        in_specs=[
            pl.BlockSpec((1, tm, d), lambda bi, i: (bi, i, 0)),
            pl.BlockSpec((1, 8, d), lambda bi, i: (bi, jnp.maximum(i * (tm // 8) - 1, 0), 0)),
            pl.BlockSpec((1, d), lambda bi, i: (0, 0)),
            pl.BlockSpec((1, d), lambda bi, i: (0, 0)),
            pl.BlockSpec((d, 2 * f), whole),
            pl.BlockSpec((8, f), whole),
            pl.BlockSpec((f, d), whole),
        ],
        out_specs=pl.BlockSpec((1, tm, d), lambda bi, i: (bi, i, 0)),
        out_shape=jax.ShapeDtypeStruct((b, s, d), F32),
        scratch_shapes=[pltpu.VMEM((tm + HALO, d), BF16),
                        pltpu.VMEM((FFN_SLOTS, tm + HALO, FFN_CHUNK), F32),
                        pltpu.VMEM((FFN_SLOTS, tm + HALO, FFN_CHUNK), F32)],
        compiler_params=_params("arbitrary", "arbitrary"),
        name="conv_ffn",
    )(x, x, g.reshape(1, d), g_final.reshape(1, d), w_up, cw, w_down)


def _pack_ffn_weights(w_up, conv_w, conv_b, w_down):
    f = w_down.shape[0]
    assert f % FFN_CHUNK == 0
    cw = jnp.concatenate([conv_w[:, :f], conv_b[None, :f], conv_w[:, f:], conv_b[None, f:]], axis=0)
    return w_up.astype(BF16), cw, w_down.astype(BF16)


def _token_mixers(x, g, w_in, fox_b, tiles):
    w_nat, w_t = _pack_in_weights(w_in)
    main, kv, fxf, vt, iw = _proj(x, g, w_nat, w_t)
    pk, pq = _fox_c(fxf, fox_b)
    return (_sb(main, vt, tiles), _moba(main, vt, tiles), _fox(main, pq, pk, vt, tiles),
            _dsa(main, kv, vt, iw, tiles))


def kernel(x, norm_mix_g, norm_ffn_g, norm_final_g, w_in, fox_b_f, t5_bias, w_gate, w_branch, w_out,
           w_up, conv_w, conv_b, w_down):
    b, s, d = x.shape
    assert s % TQ == 0 and TQ == MOBA_BLOCK and s // MOBA_BLOCK <= 8
    tiles = _make_tiles(t5_bias)
    depth = w_in.shape[0]
    for l in range(depth):
        outs = _token_mixers(x, norm_mix_g[l], w_in[l], fox_b_f[l], tiles)
        outs = [o.reshape(b * s, BRANCH_W) for o in outs]
        x = _combine(x.reshape(b * s, d), norm_mix_g[l], outs, w_gate[l].astype(BF16),
                     w_branch[l].astype(BF16), w_out[l].astype(BF16)).reshape(b, s, d)
        x = _ffn(x, norm_ffn_g[l], norm_final_g, l == depth - 1,
                 *_pack_ffn_weights(w_up[l], conv_w[l], conv_b[l], w_down[l]))
    return x
```

```python
import functools
import math

import jax
import jax.numpy as jnp
from jax import lax
from jax.experimental import pallas as pl
from jax.experimental.pallas import tpu as pltpu

F32 = jnp.float32
BF16 = jnp.bfloat16
I32 = jnp.int32

HEAD_DIM = 64
N_HEADS = 4
BRANCH_W = N_HEADS * HEAD_DIM
N_BRANCH = 4
MOBA_BLOCK = 256
MOBA_TOPK = 3
DSA_TOPK = 256
IDX_HEADS = 8
IDX_DIM = 32
T5_BUCKETS = 32
T5_MAX_DIST = 128
CONV_W = 3
RMS_EPS = 1e-6
NEG_BIG = -1e30
INT_MIN = -2147483648
LOG2E = math.log2(math.e)

TQ = 256
TK = 128
Q_PER_K = TQ // TK
T5_CONST_DIST = 113
N_DIAG = (TQ + T5_CONST_DIST - 2) // TK + 1
FAR_DIST = 1 << 20

KIND_CAUSAL = 8
KIND_STRICT_ADD = 10
KIND_STRICT_MUL = 11
N_KINDS = 12

VMEM_LIMIT = 56 * 1024 * 1024


def _dot(a, b):
    return jnp.dot(a, b, preferred_element_type=F32)


def _dot_nt(a, b):
    return lax.dot_general(a, b, (((1,), (1,)), ((), ())), preferred_element_type=F32)


def _rmsnorm(x, g):
    return x * lax.rsqrt(jnp.mean(x * x, axis=-1, keepdims=True) + RMS_EPS) * g


def _softplus(z):
    return jnp.maximum(z, 0.0) + jnp.log1p(jnp.exp(-jnp.abs(z)))


def _params(*sem):
    return pltpu.CompilerParams(dimension_semantics=sem, vmem_limit_bytes=VMEM_LIMIT)


def _tiles_kernel(tab_ref, out_ref):
    kind = pl.program_id(0)
    d = pl.program_id(1)
    sl = lax.broadcasted_iota(I32, (TK, TQ), 0)
    tl = lax.broadcasted_iota(I32, (TK, TQ), 1)
    dist = jnp.where(d == N_DIAG, FAR_DIST, (TK - TQ) + d * TK + tl - sl)
    n = jnp.maximum(dist, 0)
    max_exact = T5_BUCKETS // 2
    nf = jnp.maximum(n, 1).astype(F32)
    large = max_exact + (jnp.log(nf / max_exact) / math.log(T5_MAX_DIST / max_exact)
                         * (T5_BUCKETS - max_exact)).astype(I32)
    large = jnp.minimum(large, T5_BUCKETS - 1)
    bucket = jnp.where(n < max_exact, n, large)
    h = jnp.minimum(kind, 2 * N_HEADS - 1)
    bias = jnp.zeros((TK, TQ), F32)
    for b in range(T5_BUCKETS):
        bias = jnp.where(bucket == b, tab_ref[b, h], bias)
    causal = dist >= 0
    strict = dist > 0
    v_bias = jnp.where(causal, bias, NEG_BIG)
    v_causal = jnp.where(causal, 0.0, NEG_BIG)
    v_sadd = jnp.where(strict, 0.0, NEG_BIG)
    v_smul = jnp.where(strict, 1.0, 0.0)
    out_ref[0, 0] = jnp.where(kind < KIND_CAUSAL, v_bias,
                              jnp.where(kind < KIND_STRICT_ADD, v_causal,
                                        jnp.where(kind == KIND_STRICT_ADD, v_sadd, v_smul)))


def _make_tiles(t5_bias):
    return pl.pallas_call(
        _tiles_kernel,
        grid=(N_KINDS, N_DIAG + 1),
        in_specs=[pl.BlockSpec(memory_space=pltpu.SMEM)],
        out_specs=pl.BlockSpec((1, 1, TK, TQ), lambda k, d: (k, d, 0, 0)),
        out_shape=jax.ShapeDtypeStruct((N_KINDS, N_DIAG + 1, TK, TQ), F32),
        compiler_params=_params("arbitrary", "arbitrary"),
        name="t5_tiles",
    )(t5_bias)


N_MAIN = 8 * BRANCH_W
FOX_EXTRA = 6
N_NAT = N_MAIN + 128 + BRANCH_W
N_VT = 3 * BRANCH_W + HEAD_DIM
N_T = N_VT + 16
PROJ_CHUNK = 512


def _proj_kernel(x_ref, g_ref, w_ref, wt_ref, main_ref, kv_ref, fxf_ref, vt_ref, iw_ref):
    xn = _rmsnorm(x_ref[0], g_ref[...]).astype(BF16)
    for c in range(0, N_MAIN, PROJ_CHUNK):
        main_ref[0, :, c:c + PROJ_CHUNK] = _dot(xn, w_ref[:, c:c + PROJ_CHUNK]).astype(BF16)
    r = _dot(xn, w_ref[:, N_MAIN:N_NAT])
    kv_ref[0] = r[:, :128].astype(BF16)
    fxf_ref[0] = r[:, 128:]
    t = _dot_nt(wt_ref[...], xn)
    vt_ref[0] = t[:N_VT].astype(BF16)
    iw_ref[0] = t[N_VT:]


def _proj(x, g, w_nat, w_t):
    b, s, d = x.shape
    tm = min(512, s)
    return pl.pallas_call(
        _proj_kernel,
        grid=(b, s // tm),
        in_specs=[
            pl.BlockSpec((1, tm, d), lambda bi, i: (bi, i, 0)),
            pl.BlockSpec((1, d), lambda bi, i: (0, 0)),
            pl.BlockSpec((d, N_NAT), lambda bi, i: (0, 0)),
            pl.BlockSpec((N_T, d), lambda bi, i: (0, 0)),
        ],
        out_specs=[
            pl.BlockSpec((1, tm, N_MAIN), lambda bi, i: (bi, i, 0)),
            pl.BlockSpec((1, tm, 128), lambda bi, i: (bi, i, 0)),
            pl.BlockSpec((1, tm, BRANCH_W), lambda bi, i: (bi, i, 0)),
            pl.BlockSpec((1, N_VT, tm), lambda bi, i: (bi, 0, i)),
            pl.BlockSpec((1, 16, tm), lambda bi, i: (bi, 0, i)),
        ],
        out_shape=[
            jax.ShapeDtypeStruct((b, s, N_MAIN), BF16),
            jax.ShapeDtypeStruct((b, s, 128), BF16),
            jax.ShapeDtypeStruct((b, s, BRANCH_W), F32),
            jax.ShapeDtypeStruct((b, N_VT, s), BF16),
            jax.ShapeDtypeStruct((b, 16, s), F32),
        ],
        compiler_params=_params("arbitrary", "arbitrary"),
        name="norm_in_proj",
    )(x, g.reshape(1, d), w_nat, w_t)


def _pack_in_weights(w_in):
    d = w_in.shape[0]
    sizes = ([BRANCH_W] * 9 + [N_HEADS] + [BRANCH_W, HEAD_DIM, HEAD_DIM]
             + [IDX_HEADS * IDX_DIM, IDX_DIM, IDX_HEADS])
    offs = [0]
    for sz in sizes:
        offs.append(offs[-1] + sz)
    seg = [w_in[:, offs[i]:offs[i + 1]] for i in range(len(sizes))]
    (sb_q, sb_k, sb_v, mb_q, mb_k, mb_v, fx_q, fx_k, fx_v, fx_f,
     ds_q, ds_k, ds_v, ix_q, ix_k, ix_w) = seg
    scale = HEAD_DIM ** -0.5
    z = lambda n: jnp.zeros((d, n), w_in.dtype)
    w_nat = jnp.concatenate(
        [sb_q * scale, sb_k, mb_q * scale, mb_k, fx_q * scale, fx_k, ds_q * scale, ix_q,
         ds_k, ix_k, z(128 - HEAD_DIM - IDX_DIM)]
        + [w for h in range(N_HEADS) for w in [fx_f[:, h:h + 1]] * FOX_EXTRA + [z(HEAD_DIM - FOX_EXTRA)]],
        axis=1).astype(BF16)
    w_t = jnp.concatenate([sb_v, mb_v, fx_v, ds_v, ix_w, z(16 - IDX_HEADS)], axis=1).T.astype(BF16)
    return w_nat, w_t


CUM_BLOCK = 256


def _fox_c_kernel(f_ref, b_ref, pk_ref, pq_ref):
    s = f_ref.shape[1]
    row = lax.broadcasted_iota(I32, (CUM_BLOCK, CUM_BLOCK), 0)
    col = lax.broadcasted_iota(I32, (CUM_BLOCK, CUM_BLOCK), 1)
    tri = jnp.where(row >= col, 1.0, 0.0).astype(BF16)
    tri3 = jnp.concatenate([tri, tri, tri], axis=1)
    lane = lax.broadcasted_iota(I32, (CUM_BLOCK, BRANCH_W), 1) & (HEAD_DIM - 1)
    carry = jnp.zeros((1, BRANCH_W), F32)
    for blk in range(s // CUM_BLOCK):
        rows = slice(blk * CUM_BLOCK, (blk + 1) * CUM_BLOCK)
        f = f_ref[0, rows, :] + b_ref[...]
        ls = -_softplus(-f)
        hi = ls.astype(BF16)
        r1 = ls - hi.astype(F32)
        mid = r1.astype(BF16)
        lo = (r1 - mid.astype(F32)).astype(BF16)
        c = _dot(tri3, jnp.concatenate([hi, mid, lo], axis=0)) + carry
        carry = c[CUM_BLOCK - 1:CUM_BLOCK, :]
        c_hi = c.astype(BF16).astype(F32)
        r2 = c - c_hi
        c_mid = r2.astype(BF16).astype(F32)
        c_lo = r2 - c_mid
        third = jnp.where(lane >= 3, lane - 3, lane)
        piece = jnp.where(third == 0, c_hi, jnp.where(third == 1, c_mid, c_lo))
        pk = jnp.where(lane < 3, piece, jnp.where(lane < FOX_EXTRA, 1.0, 0.0))
        pq = jnp.where(lane < 3, -1.0, jnp.where(lane < FOX_EXTRA, piece, 0.0))
        pk_ref[0, rows, :] = pk.astype(BF16)
        pq_ref[0, rows, :] = pq.astype(BF16)


def _fox_c(fxf, fox_b):
    b, s, _ = fxf.shape
    bias = jnp.concatenate([v for h in range(N_HEADS)
                            for v in [fox_b[h:h + 1]] * FOX_EXTRA + [jnp.zeros((HEAD_DIM - FOX_EXTRA,), F32)]])
    spec = pl.BlockSpec((1, s, BRANCH_W), lambda bi: (bi, 0, 0))
    return pl.pallas_call(
        _fox_c_kernel,
        grid=(b,),
        in_specs=[spec, pl.BlockSpec((1, BRANCH_W), lambda bi: (0, 0))],
        out_specs=[spec, spec],
        out_shape=[jax.ShapeDtypeStruct((b, s, BRANCH_W), BF16)] * 2,
        compiler_params=_params("arbitrary"),
        name="fox_cumgate",
    )(fxf, bias.reshape(1, BRANCH_W))


def _tile_pos(j, jmax):
    return jnp.minimum(jmax - j, N_DIAG), pl.multiple_of(j * TK, TK)


def _tile_loop(n_tiles, body, init, first=0):
    def trip(t, carry):
        for u in range(Q_PER_K):
            carry = body(t * Q_PER_K + u, carry)
        return carry

    return lax.fori_loop(first // Q_PER_K, n_tiles // Q_PER_K, trip, init)


def _fold8(x, op):
    return op(x.reshape(TK // 8, 8, TQ), axis=0)


def _align_heads(dst_s, src, width=HEAD_DIM):
    for h in range(src.shape[1] // width):
        dst_s[:, h * 128:h * 128 + width] = src[:, h * width:(h + 1) * width]


def _head(h, width=HEAD_DIM):
    return slice(h * 128, h * 128 + width)


def _neg_rows():
    return tuple(jnp.full((8, TQ), -1e38, F32) for _ in range(N_HEADS))


def _weights_matmul(w_s, vt_ref, o_ref, acc_ref, n_tiles, inv_l, shared_v):
    s = w_s.shape[1]
    half = s // 2
    use_second = n_tiles * TK > half

    def zero_body(j, carry):
        s0 = pl.multiple_of(j * TK, TK)
        for h in range(N_HEADS):
            w_s[h, pl.ds(s0, TK), :] = jnp.zeros((TK, TQ), BF16)
        return carry

    lax.fori_loop(n_tiles, jnp.where(use_second, s // TK, half // TK), zero_body, 0)

    def product(lo, first):
        for h in range(N_HEADS):
            rows = slice(h * HEAD_DIM, (h + 1) * HEAD_DIM)
            vt = vt_ref[0, :, lo:lo + half] if shared_v else vt_ref[0, rows, lo:lo + half]
            out = _dot(vt, w_s[h, lo:lo + half, :])
            acc_ref[rows, :] = out if first else acc_ref[rows, :] + out

    product(0, True)
    pl.when(use_second)(functools.partial(product, half, False))
    if inv_l is not None:
        for h in range(N_HEADS):
            rows = slice(h * HEAD_DIM, (h + 1) * HEAD_DIM)
            acc_ref[rows, :] = acc_ref[rows, :] * inv_l[h]
    o_ref[0] = acc_ref[...].T.astype(o_ref.dtype)


def _softmax_passes(z_s, p_s, maxes, vt_ref, o_ref, acc_ref, n_tiles, shared_v=False):
    m = [jnp.max(mx, axis=0, keepdims=True) for mx in maxes]

    def exp_body(j, sums):
        s0 = pl.multiple_of(j * TK, TK)
        out = []
        for h in range(N_HEADS):
            p = jnp.exp2(z_s[h, pl.ds(s0, TK), :] - m[h])
            p_s[h, pl.ds(s0, TK), :] = p.astype(BF16)
            out.append(sums[h] + _fold8(p, jnp.sum))
        return tuple(out)

    sums = lax.fori_loop(0, n_tiles, exp_body, tuple(jnp.zeros((8, TQ), F32) for _ in range(N_HEADS)))
    inv_l = [1.0 / jnp.sum(sm, axis=0, keepdims=True) for sm in sums]
    _weights_matmul(p_s, vt_ref, o_ref, acc_ref, n_tiles, inv_l, shared_v)


def _sb_kernel(q_ref, k_ref, vt_ref, tb_ref, o_ref, acc_ref, base_s, hl_s, w_s):
    i = pl.program_id(1)
    jmax = (i + 1) * Q_PER_K - 1

    def score_body(masked, j, carry):
        d, s0 = _tile_pos(j, jmax)
        for h in range(N_HEADS):
            cols = slice(h * HEAD_DIM, (h + 1) * HEAD_DIM)
            z = _dot_nt(k_ref[0, pl.ds(s0, TK), cols], q_ref[0, :, cols])
            sp = jnp.maximum(z, 0.0) + jnp.log(1.0 + jnp.exp(-jnp.abs(z)))
            logsig = z - sp
            base_s[h, pl.ds(s0, TK), :] = logsig + tb_ref[0, d] if masked else logsig
            pos = sp * tb_ref[1, d] if masked else sp
            hi = pos.astype(BF16)
            lo = (pos - hi.astype(F32)).astype(BF16)
            hl_s[h, j] = jnp.concatenate([hi, lo], axis=0)
        return carry

    n_tiles = jmax + 1
    n_plain = jnp.maximum(n_tiles - Q_PER_K * ((N_DIAG + Q_PER_K - 1) // Q_PER_K), 0)
    _tile_loop(n_plain, functools.partial(score_body, False), 0)
    _tile_loop(n_tiles, functools.partial(score_body, True), 0, first=n_plain)

    row = lax.broadcasted_iota(I32, (TK + 16, 2 * TK), 0)
    col = lax.broadcasted_iota(I32, (TK + 16, 2 * TK), 1)
    upper = jnp.where(((col & (TK - 1)) > row) | (row >= TK), -1.0, 0.0).astype(BF16)

    def weight_body(jj, suffix):
        j = jmax - jj
        s0 = pl.multiple_of(j * TK, TK)
        out = []
        for h in range(N_HEADS):
            r = _dot(upper, hl_s[h, j])
            w = jnp.exp(base_s[h, pl.ds(s0, TK), :] + r[:TK] + suffix[h])
            w_s[h, pl.ds(s0, TK), :] = w.astype(BF16)
            out.append(suffix[h] + r[TK:TK + 1])
        return tuple(out)

    _tile_loop(jmax + 1, weight_body, tuple(jnp.zeros((1, TQ), F32) for _ in range(N_HEADS)))
    _weights_matmul(w_s, vt_ref, o_ref, acc_ref, jmax + 1, None, False)


def _moba_kernel(q_ref, k_ref, vt_ref, tb_ref, o_ref, acc_ref, z_s, p_s, selb_ref, km_s, qx_s, kx_s):
    i = pl.program_id(1)
    own = i
    jmax = (i + 1) * Q_PER_K - 1
    nb = k_ref.shape[1] // MOBA_BLOCK
    _align_heads(qx_s, q_ref[0])
    pl.when(i == 0)(lambda: _align_heads(kx_s, k_ref[0]))

    @pl.when(i == 0)
    def _():
        s_len = k_ref.shape[1]
        blk = lax.broadcasted_iota(I32, (8, s_len), 1) >> (MOBA_BLOCK.bit_length() - 1)
        avg = jnp.where(blk == lax.broadcasted_iota(I32, (8, s_len), 0), 1.0 / MOBA_BLOCK, 0.0)
        km_s[...] = _dot(avg.astype(BF16), k_ref[0])

    km_all = km_s[...]
    n_idx = lax.broadcasted_iota(I32, (8, TQ), 0)
    for h in range(N_HEADS):
        cols = slice(h * HEAD_DIM, (h + 1) * HEAD_DIM)
        q_h = q_ref[0, :, cols]
        km = km_all[:, cols]
        km_hi = km.astype(BF16)
        km_lo = (km - km_hi.astype(F32)).astype(BF16)
        gate = _dot_nt(km_hi, q_h) + _dot_nt(km_lo, q_h)
        rank = jnp.zeros((8, TQ), F32)
        for m in range(nb):
            g_m = gate[m:m + 1, :]
            wins = jnp.where(n_idx > m, jnp.where(g_m >= gate, 1.0, 0.0), jnp.where(g_m > gate, 1.0, 0.0))
            rank = rank + wins * jnp.where(m < own, 1.0, 0.0)
        chosen = jnp.where(rank < MOBA_TOPK, 0.0, NEG_BIG)
        selb_ref[h] = jnp.where(n_idx < own, chosen, jnp.where(n_idx == own, 0.0, NEG_BIG))

    def score_body(j, maxes):
        d, s0 = _tile_pos(j, jmax)
        n = j // (MOBA_BLOCK // TK)
        out = []
        for h in range(N_HEADS):
            z = _dot_nt(kx_s[pl.ds(s0, TK), _head(h)], qx_s[:, _head(h)])
            z = (z + tb_ref[h, d] + selb_ref[h, pl.ds(n, 1), :]) * LOG2E
            z_s[h, pl.ds(s0, TK), :] = z
            out.append(jnp.maximum(maxes[h], _fold8(z, jnp.max)))
        return tuple(out)

    maxes = _tile_loop(jmax + 1, score_body, _neg_rows())
    _softmax_passes(z_s, p_s, maxes, vt_ref, o_ref, acc_ref, jmax + 1)


def _fox_kernel(q_ref, k_ref, pq_ref, pk_ref, vt_ref, tb_ref, o_ref, acc_ref, z_s, p_s, qx_s, kx_s):
    i = pl.program_id(1)
    jmax = (i + 1) * Q_PER_K - 1

    for h in range(N_HEADS):
        cols = slice(h * HEAD_DIM, (h + 1) * HEAD_DIM)
        qx_s[:, h * 128:h * 128 + HEAD_DIM] = q_ref[0, :, cols]
        qx_s[:, h * 128 + HEAD_DIM:(h + 1) * 128] = pq_ref[0, :, cols]

    @pl.when(i == 0)
    def _():
        for h in range(N_HEADS):
            cols = slice(h * HEAD_DIM, (h + 1) * HEAD_DIM)
            kx_s[:, h * 128:h * 128 + HEAD_DIM] = k_ref[0, :, cols]
            kx_s[:, h * 128 + HEAD_DIM:(h + 1) * 128] = pk_ref[0, :, cols]

    def score_body(j, maxes):
        d, s0 = _tile_pos(j, jmax)
        causal = tb_ref[0, d]
        out = []
        for h in range(N_HEADS):
            ext = slice(h * 128, (h + 1) * 128)
            z = (_dot_nt(kx_s[pl.ds(s0, TK), ext], qx_s[:, ext]) + causal) * LOG2E
            z_s[h, pl.ds(s0, TK), :] = z
            out.append(jnp.maximum(maxes[h], _fold8(z, jnp.max)))
        return tuple(out)

    maxes = _tile_loop(jmax + 1, score_body, _neg_rows())
    _softmax_passes(z_s, p_s, maxes, vt_ref, o_ref, acc_ref, jmax + 1)


I16 = jnp.int16
HALF = 1 << 15


def _count16(ref, n_tiles, thr16, strict):
    def body(j, cnt):
        t = ref[j]
        hit = (t > thr16) if strict else (t >= thr16)
        g = jnp.where(hit, jnp.int16(1), jnp.int16(0)).reshape(TK // 16, 16, TQ)
        for r in range(TK // 16):
            cnt = cnt + g[r]
        return cnt

    cnt = _tile_loop(n_tiles, body, jnp.zeros((16, TQ), I16))
    return jnp.sum(cnt.astype(I32), axis=0, keepdims=True)


def _radix16(ref, n_tiles, need):
    def bit_body(it, t_u):
        cand = t_u | (jnp.int32(1) << (15 - it))
        cnt = _count16(ref, n_tiles, (cand - HALF).astype(I16), False)
        return jnp.where(cnt >= need, cand, t_u)

    return lax.fori_loop(0, 16, bit_body, jnp.zeros((1, TQ), I32))


def _dsa_kernel(q_ref, iq_ref, kv_ref, vt_ref, iw_ref, tb_ref, tc_ref, o_ref,
                acc_ref, z_s, p_s, keys_ref, khi_ref, klo_ref, qx_s, iqx_s, *, topk):
    i = pl.program_id(1)
    jmax = (i + 1) * Q_PER_K - 1
    _align_heads(qx_s, q_ref[0])
    _align_heads(iqx_s, iq_ref[0], IDX_DIM)

    def index_body(j, carry):
        d, s0 = _tile_pos(j, jmax)
        k_idx = kv_ref[0, pl.ds(s0, TK), HEAD_DIM:HEAD_DIM + IDX_DIM]
        sc = jnp.zeros((TK, TQ), F32)
        for hh in range(IDX_HEADS):
            qi = iqx_s[:, _head(hh, IDX_DIM)]
            sc = sc + iw_ref[0, hh:hh + 1, :] * jnp.maximum(_dot_nt(k_idx, qi), 0.0)
        sc = jnp.where(sc == 0.0, 0.0, sc)
        bits = lax.bitcast_convert_type(sc, I32)
        key = bits ^ ((bits >> 31) & 0x7FFFFFFF)
        key = jnp.where(tc_ref[0, d] < 0.0, INT_MIN, key)
        keys_ref[j] = key
        khi_ref[j] = (key >> 16).astype(I16)
        klo_ref[j] = ((key & 0xFFFF) - HALF).astype(I16)
        return carry

    _tile_loop(jmax + 1, index_body, 0)

    n_tiles = jmax + 1
    t_hi = _radix16(khi_ref, n_tiles, topk)
    t_hi16 = (t_hi - HALF).astype(I16)
    need_lo = topk - _count16(khi_ref, n_tiles, t_hi16, True)

    def restrict_body(j, carry):
        klo_ref[j] = jnp.where(khi_ref[j] == t_hi16, klo_ref[j], jnp.int16(-HALF))
        return carry

    _tile_loop(n_tiles, restrict_body, 0)
    t_lo = _radix16(klo_ref, n_tiles, need_lo)
    thr = (t_hi - HALF) * (1 << 16) + t_lo
    n_above = (topk - need_lo) + _count16(klo_ref, n_tiles, (t_lo - HALF).astype(I16), True)
    need_eq = (topk - n_above).astype(F32)

    row = lax.broadcasted_iota(I32, (TK + 16, TK), 0)
    col = lax.broadcasted_iota(I32, (TK + 16, TK), 1)
    lower = jnp.where((col < row) | (row >= TK), 1.0, 0.0).astype(BF16)

    def score_body(j, carry):
        maxes, seen = carry
        d, s0 = _tile_pos(j, jmax)
        k_j = kv_ref[0, pl.ds(s0, TK), 0:HEAD_DIM]
        key = keys_ref[j]
        is_thr = key == thr
        r = _dot(lower, jnp.where(is_thr, 1.0, 0.0).astype(BF16))
        tie = jnp.where(r[:TK] + seen < need_eq, 0.0, NEG_BIG)
        selb = jnp.where(key > thr, 0.0, jnp.where(is_thr, tie, NEG_BIG))
        out = []
        for h in range(N_HEADS):
            z = (_dot_nt(k_j, qx_s[:, _head(h)]) + tb_ref[h, d] + selb) * LOG2E
            z_s[h, pl.ds(s0, TK), :] = z
            out.append(jnp.maximum(maxes[h], _fold8(z, jnp.max)))
        return tuple(out), seen + r[TK:TK + 1]

    maxes, _ = _tile_loop(jmax + 1, score_body, (_neg_rows(), jnp.zeros((1, TQ), F32)))
    _softmax_passes(z_s, p_s, maxes, vt_ref, o_ref, acc_ref, jmax + 1, shared_v=True)


def _mixer_call(kernel_fn, name, b, s, in_specs, scratch, args):
    return pl.pallas_call(
        kernel_fn,
        grid=(b, s // TQ),
        in_specs=in_specs,
        out_specs=pl.BlockSpec((1, TQ, BRANCH_W), lambda bi, i: (bi, i, 0)),
        out_shape=jax.ShapeDtypeStruct((b, s, BRANCH_W), BF16),
        scratch_shapes=scratch,
        compiler_params=_params("arbitrary", "arbitrary"),
        name=name,
    )(*args)


def _q_spec(width, col):
    return pl.BlockSpec((1, TQ, width), lambda bi, i: (bi, i, col))


def _k_spec(s, width, col):
    return pl.BlockSpec((1, s, width), lambda bi, i: (bi, 0, col))


def _vt_spec(s, rows, row_blk):
    return pl.BlockSpec((1, rows, s), lambda bi, i: (bi, row_blk, 0))


def _tile_spec(n, blk):
    return pl.BlockSpec((n, N_DIAG + 1, TK, TQ), lambda bi, i: (blk, 0, 0, 0))


_ACC = lambda: pltpu.VMEM((BRANCH_W, TQ), F32)
_SCORES = lambda s: pltpu.VMEM((N_HEADS, s, TQ), F32)
_WEIGHTS = lambda s: pltpu.VMEM((N_HEADS, s, TQ), BF16)
_HEADS128 = lambda rows: pltpu.VMEM((rows, N_HEADS * 128), BF16)


def _sb(main, vt, tiles):
    b, s, _ = main.shape
    return _mixer_call(
        _sb_kernel, "sb_mixer", b, s,
        [_q_spec(BRANCH_W, 0), _k_spec(s, BRANCH_W, 1), _vt_spec(s, BRANCH_W, 0),
         _tile_spec(2, KIND_STRICT_ADD // 2)],
        [_ACC(), _SCORES(s), pltpu.VMEM((N_HEADS, s // TK, 2 * TK, TQ), BF16), _WEIGHTS(s)],
        (main, main, vt, tiles))


def _moba(main, vt, tiles):
    b, s, _ = main.shape
    return _mixer_call(
        _moba_kernel, "moba_mixer", b, s,
        [_q_spec(BRANCH_W, 2), _k_spec(s, BRANCH_W, 3), _vt_spec(s, BRANCH_W, 1),
         _tile_spec(N_HEADS, 0)],
        [_ACC(), _SCORES(s), _WEIGHTS(s), pltpu.VMEM((N_HEADS, 8, TQ), F32),
         pltpu.VMEM((8, BRANCH_W), F32), _HEADS128(TQ), _HEADS128(s)], (main, main, vt, tiles))


def _fox(main, pq, pk, vt, tiles):
    b, s, _ = main.shape
    return _mixer_call(
        _fox_kernel, "fox_mixer", b, s,
        [_q_spec(BRANCH_W, 4), _k_spec(s, BRANCH_W, 5), _q_spec(BRANCH_W, 0), _k_spec(s, BRANCH_W, 0),
         _vt_spec(s, BRANCH_W, 2), _tile_spec(1, KIND_CAUSAL)],
        [_ACC(), _SCORES(s), _WEIGHTS(s), _HEADS128(TQ), _HEADS128(s)],
        (main, main, pq, pk, vt, tiles))


def _dsa(main, kv, vt, iw, tiles):
    b, s, _ = main.shape
    topk = min(DSA_TOPK, s // 4)
    return _mixer_call(
        functools.partial(_dsa_kernel, topk=topk), "dsa_mixer", b, s,
        [_q_spec(BRANCH_W, 6), _q_spec(BRANCH_W, 7), _k_spec(s, 128, 0),
         _vt_spec(s, HEAD_DIM, 3 * BRANCH_W // HEAD_DIM),
         pl.BlockSpec((1, 16, TQ), lambda bi, i: (bi, 0, i)),
         _tile_spec(N_HEADS, 1), _tile_spec(1, KIND_CAUSAL)],
        [_ACC(), _SCORES(s), _WEIGHTS(s), pltpu.VMEM((s // TK, TK, TQ), I32),
         pltpu.VMEM((s // TK, TK, TQ), I16), pltpu.VMEM((s // TK, TK, TQ), I16), _HEADS128(TQ),
         pltpu.VMEM((TQ, IDX_HEADS * 128), BF16)],
        (main, main, kv, vt, iw, tiles, tiles))


def _combine_kernel(x_ref, g_ref, o0, o1, o2, o3, wg_ref, wb_ref, wo_ref, out_ref):
    x = x_ref[...]
    xn = _rmsnorm(x, g_ref[...]).astype(BF16)
    mixed = None
    for i, o_ref in enumerate((o0, o1, o2, o3)):
        gate = 1.0 / (1.0 + jnp.exp(-_dot(xn, wg_ref[i])))
        y = gate * _dot(o_ref[...], wb_ref[i])
        mixed = y if mixed is None else mixed + y
    out_ref[...] = x + _dot(mixed.astype(BF16), wo_ref[...])


def _combine(x2, g, outs, wg, wb, wo):
    t, d = x2.shape
    tm = 256
    row = lambda i: (i, 0)
    return pl.pallas_call(
        _combine_kernel,
        grid=(t // tm,),
        in_specs=[pl.BlockSpec((tm, d), row), pl.BlockSpec((1, d), lambda i: (0, 0))]
        + [pl.BlockSpec((tm, BRANCH_W), row)] * N_BRANCH
        + [pl.BlockSpec((N_BRANCH, d, d), lambda i: (0, 0, 0)),
           pl.BlockSpec((N_BRANCH, BRANCH_W, d), lambda i: (0, 0, 0)),
           pl.BlockSpec((d, d), lambda i: (0, 0))],
        out_specs=pl.BlockSpec((tm, d), row),
        out_shape=jax.ShapeDtypeStruct((t, d), F32),
        compiler_params=_params("arbitrary"),
        name="gate_combine_out",
    )(x2, g.reshape(1, d), *outs, wg, wb, wo)


FFN_CHUNK = 256
FFN_SLOTS = 4
HALO = 16


def _gelu(g):
    return 0.5 * g * (1.0 + lax.erf(g * (2.0 ** -0.5)))


def _ffn_kernel(x_ref, xp_ref, g_ref, gf_ref, wup_ref, cw_ref, wd_ref, o_ref, xn_s, hg_s, hu_s, *, final_norm):
    i = pl.program_id(1)
    tm = x_ref.shape[1]
    x = x_ref[0]
    gain = g_ref[...]
    prev = _rmsnorm(xp_ref[0], gain) * jnp.where(i > 0, 1.0, 0.0)
    xn_s[...] = jnp.concatenate([jnp.zeros_like(prev), prev, _rmsnorm(x, gain)], axis=0).astype(BF16)

    def taps(h_s, cw, r):
        out = cw[r + 3:r + 4, :]
        for j in range(CONV_W):
            out = out + cw[r + j:r + j + 1, :] * h_s[HALO - (CONV_W - 1) + j:HALO - (CONV_W - 1) + j + tm, :]
        return out

    f = wd_ref.shape[0]
    nc = f // FFN_CHUNK
    chunk = lambda c: slice(c * FFN_CHUNK, (c + 1) * FFN_CHUNK)

    def up(c):
        xn = xn_s[...]
        hg_s[c % FFN_SLOTS] = _dot(xn, wup_ref[:, chunk(c)])
        hu_s[c % FFN_SLOTS] = _dot(xn, wup_ref[:, f + c * FFN_CHUNK:f + (c + 1) * FFN_CHUNK])

    for c in range(FFN_SLOTS - 1):
        up(c)
    acc = x
    prev = None
    for c in range(nc):
        if c + FFN_SLOTS - 1 < nc:
            up(c + FFN_SLOTS - 1)
        if prev is not None:
            acc = acc + _dot(prev, wd_ref[chunk(c - 1), :])
        cw = cw_ref[:, chunk(c)]
        slot = c % FFN_SLOTS
        prev = (_gelu(taps(hg_s.at[slot], cw, 0)) * taps(hu_s.at[slot], cw, 4)).astype(BF16)
    acc = acc + _dot(prev, wd_ref[chunk(nc - 1), :])
    o_ref[0] = _rmsnorm(acc, gf_ref[...]) if final_norm else acc


def _ffn(x, g, g_final, final_norm, w_up, cw, w_down):
    b, s, d = x.shape
    tm = 256
    f = w_down.shape[0]
    whole = lambda bi, i: (0, 0)
    return pl.pallas_call(
        functools.partial(_ffn_kernel, final_norm=final_norm),
        grid=(b, s // tm),
        in_specs=[
            pl.BlockSpec((1, tm, d), lambda bi, i: (bi, i, 0)),
            pl.BlockSpec((1, 8, d), lambda bi, i: (bi, jnp.maximum(i * (tm // 8) - 1, 0), 0)),
            pl.BlockSpec((1, d), lambda bi, i: (0, 0)),
            pl.BlockSpec((1, d), lambda bi, i: (0, 0)),
            pl.BlockSpec((d, 2 * f), whole),
            pl.BlockSpec((8, f), whole),
            pl.BlockSpec((f, d), whole),
        ],
        out_specs=pl.BlockSpec((1, tm, d), lambda bi, i: (bi, i, 0)),
        out_shape=jax.ShapeDtypeStruct((b, s, d), F32),
        scratch_shapes=[pltpu.VMEM((tm + HALO, d), BF16),
                        pltpu.VMEM((FFN_SLOTS, tm + HALO, FFN_CHUNK), F32),
                        pltpu.VMEM((FFN_SLOTS, tm + HALO, FFN_CHUNK), F32)],
        compiler_params=_params("arbitrary", "arbitrary"),
        name="conv_ffn",
    )(x, x, g.reshape(1, d), g_final.reshape(1, d), w_up, cw, w_down)


def _pack_ffn_weights(w_up, conv_w, conv_b, w_down):
    f = w_down.shape[0]
    assert f % FFN_CHUNK == 0
    cw = jnp.concatenate([conv_w[:, :f], conv_b[None, :f], conv_w[:, f:], conv_b[None, f:]], axis=0)
    return w_up.astype(BF16), cw, w_down.astype(BF16)


def _token_mixers(x, g, w_in, fox_b, tiles):
    w_nat, w_t = _pack_in_weights(w_in)
    main, kv, fxf, vt, iw = _proj(x, g, w_nat, w_t)
    pk, pq = _fox_c(fxf, fox_b)
    return (_sb(main, vt, tiles), _moba(main, vt, tiles), _fox(main, pq, pk, vt, tiles),
            _dsa(main, kv, vt, iw, tiles))


def kernel(x, norm_mix_g, norm_ffn_g, norm_final_g, w_in, fox_b_f, t5_bias, w_gate, w_branch, w_out,
           w_up, conv_w, conv_b, w_down):
    b, s, d = x.shape
    assert s % TQ == 0 and TQ == MOBA_BLOCK and s // MOBA_BLOCK <= 8
    tiles = _make_tiles(t5_bias)
    depth = w_in.shape[0]
    for l in range(depth):
        outs = _token_mixers(x, norm_mix_g[l], w_in[l], fox_b_f[l], tiles)
        outs = [o.reshape(b * s, BRANCH_W) for o in outs]
        x = _combine(x.reshape(b * s, d), norm_mix_g[l], outs, w_gate[l].astype(BF16),
                     w_branch[l].astype(BF16), w_out[l].astype(BF16)).reshape(b, s, d)
        x = _ffn(x, norm_ffn_g[l], norm_final_g, l == depth - 1,
                 *_pack_ffn_weights(w_up[l], conv_w[l], conv_b[l], w_down[l]))
    return x
```
